```python
import math
import jax, jax.numpy as jnp
from jax import lax
import numpy as np

D_MODEL = 1024
BATCH = 32
SEQ = 2048
DEPTH = 2
DEC_BATCH = 8
DEC_SEQ = 16
PAST_LEN = 1024

CHUNK = 64
HEAD_DIM = 64
SB_HEADS = 6
SB_WIDTH = SB_HEADS * HEAD_DIM
SB_BLOCK = 128
MLP_GROUPS = 4
MLP_WIDTH = MLP_GROUPS * HEAD_DIM
MLP_CHUNK = 128
HG_HEADS = 6
HG_KDIM = 64
HG_VDIM = 64
HG_WIDTH = HG_HEADS * HG_VDIM
HG_BLOCK = CHUNK
MIX_WIDTH = SB_WIDTH + MLP_WIDTH + HG_WIDTH
IN_WIDTH = 3 * SB_WIDTH + 2 * MLP_WIDTH + 2 * HG_HEADS * HG_KDIM + 2 * HG_WIDTH
D_FF = 4 * D_MODEL
ALPHA = (2 * DEPTH) ** 0.25
BETA_INIT = (8 * DEPTH) ** -0.25
LN_EPS = 1e-5
RMS_EPS = 1e-6

kernel_name = 'hybrid_stickbreak_gmlp_hgrn2_stream_step'


def layer_norm(x, g, b):
    xf = x.astype(jnp.float32)
    mu = jnp.mean(xf, axis=-1, keepdims=True)
    var = jnp.mean(jnp.square(xf - mu), axis=-1, keepdims=True)
    return ((xf - mu) * lax.rsqrt(var + LN_EPS) * g + b).astype(x.dtype)


def split_proj(p):
    sizes = (SB_WIDTH,) * 3 + (MLP_WIDTH,) * 2 + (HG_HEADS * HG_KDIM,) * 2 + (HG_WIDTH,) * 2
    out, off = [], 0
    for s in sizes:
        out.append(p[..., off:off + s])
        off += s
    return out


def stick_breaking(q, k, v, q_start):
    Lq, Lk = q.shape[1], k.shape[1]
    z = jnp.einsum('bqhd,bkhd->bhqk', q, k).astype(jnp.float32) / math.sqrt(HEAD_DIM)
    t_pos = q_start + jnp.arange(Lq)
    s_pos = jnp.arange(Lk)
    mask = s_pos[None, :] < t_pos[:, None]
    log_beta = jax.nn.log_sigmoid(z)
    log_keep = jnp.where(mask, jax.nn.log_sigmoid(-z), 0.0)
    later = lax.cumsum(log_keep, axis=3, reverse=True) - log_keep
    w = jnp.where(mask, jnp.exp(log_beta + later), 0.0)
    return jnp.einsum('bhqk,bkhd->bqhd', w.astype(v.dtype), v)


def stick_breaking_prompt(q, k, v):
    L = q.shape[1]
    outs = []
    for i in range(L // SB_BLOCK):
        lo, hi = i * SB_BLOCK, (i + 1) * SB_BLOCK
        outs.append(stick_breaking(q[:, lo:hi], k[:, :hi], v[:, :hi], lo))
    return jnp.concatenate(outs, axis=1)


def spatial_gate(u, v, ln_g, ln_b, w_s, b_s):
    B, L, _ = u.shape
    vn = layer_norm(v, ln_g, ln_b)
    nc = max(L // MLP_CHUNK, 1)
    cl = L // nc
    tri = jnp.tril(jnp.ones((cl, cl), dtype=bool))
    w = jnp.where(tri[None], w_s[:, :cl, :cl], 0.0)
    vc = vn.reshape(B, nc, cl, MLP_GROUPS, HEAD_DIM)
    mixed = jnp.einsum('gts,bcsgd->bctgd', w, vc) + jnp.transpose(b_s[:, :cl])[None, None, :, :, None]
    return u * mixed.reshape(B, L, MLP_WIDTH).astype(u.dtype), vn


def hgrn_lower_bounds(logits):
    g = jax.nn.softmax(logits.astype(jnp.float32), axis=0)
    c = jnp.cumsum(g, axis=0)
    return c - c[0:1]


def hgrn2_inputs(h_q, h_f, h_i, lb):
    B, L, _ = h_q.shape
    zf = h_f.astype(jnp.float32)
    log_f = jnp.logaddexp(jnp.log(lb), jnp.log1p(-lb) + jax.nn.log_sigmoid(zf))
    k = (1.0 - lb) * jax.nn.sigmoid(-zf)
    q = jax.nn.silu(h_q.astype(jnp.float32))
    v = h_i.astype(jnp.float32)
    heads = lambda a: a.reshape(B, L, HG_HEADS, -1).transpose(0, 2, 1, 3)
    return heads(q), heads(k), heads(v), heads(log_f)


def hgrn2_block(S, q, k, v, log_f):
    S = S.astype(jnp.float32)
    T = q.shape[2]
    b = jnp.cumsum(log_f, axis=2)
    causal = jnp.tril(jnp.ones((T, T), dtype=bool))
    rel = jnp.where(causal[None, None, :, :, None], b[:, :, :, None, :] - b[:, :, None, :, :], -jnp.inf)
    scores = jnp.einsum('bhtd,bhsd,bhtsd->bhts', q, k, jnp.exp(rel))
    o = jnp.einsum('bhts,bhse->bhte', scores, v) + jnp.einsum('bhtd,bhde->bhte', q * jnp.exp(b), S)
    b_last = b[:, :, -1:, :]
    S_new = jnp.exp(b_last[:, :, 0, :])[..., None] * S + jnp.einsum('bhsd,bhse->bhde', k * jnp.exp(b_last - b), v)
    return S_new, o


def hgrn2_prompt(q, k, v, log_f):
    B, H, L, _ = q.shape
    n = L // HG_BLOCK
    split = lambda a: jnp.moveaxis(a.reshape(B, H, n, HG_BLOCK, a.shape[-1]), 2, 0)
    S0 = jnp.zeros((B, H, HG_KDIM, HG_VDIM), jnp.float32)
    S, o = lax.scan(lambda s, xs: hgrn2_block(s, *xs), S0, (split(q), split(k), split(v), split(log_f)))
    return S, jnp.moveaxis(o, 0, 2).reshape(B, H, L, HG_VDIM)


def trunk(x, past_k, past_v, hg_state, w_in, w_out, mlp_ln_g, mlp_ln_b, mlp_ws, mlp_bs,
          hg_lb_logits, hg_norm_w, ln1_g, ln1_b, w_ff1, w_ff2, ln2_g, ln2_b):
    B, L, _ = x.shape
    prompt = past_k is None
    lbs = hgrn_lower_bounds(hg_lb_logits)
    new_k, new_v, new_s, new_mv = [], [], [], []
    for l in range(DEPTH):
        q_a, k_a, v_a, u_b, v_b, q_c, f_c, i_c, g_c = split_proj(x @ w_in[l])
        qa = q_a.reshape(B, L, SB_HEADS, HEAD_DIM)
        ka = k_a.reshape(B, L, SB_HEADS, HEAD_DIM)
        va = v_a.reshape(B, L, SB_HEADS, HEAD_DIM)
        if prompt:
            o_a = stick_breaking_prompt(qa, ka, va)
        else:
            o_a = stick_breaking(qa, jnp.concatenate([past_k[l].astype(ka.dtype), ka], axis=1),
                                 jnp.concatenate([past_v[l].astype(va.dtype), va], axis=1), past_k.shape[2])
        o_b, vn = spatial_gate(u_b, v_b, mlp_ln_g[l], mlp_ln_b[l], mlp_ws[l], mlp_bs[l])
        hq, hk, hv, hf = hgrn2_inputs(q_c, f_c, i_c, lbs[l])
        if prompt:
            S, o = hgrn2_prompt(hq, hk, hv, hf)
        else:
            S, o = hgrn2_block(hg_state[l], hq, hk, hv, hf)
        o = o.transpose(0, 2, 1, 3)
        o = o * lax.rsqrt(jnp.mean(o * o, axis=-1, keepdims=True) + RMS_EPS) * hg_norm_w[l].reshape(HG_HEADS, HG_VDIM)
        o_c = (o.reshape(B, L, HG_WIDTH) * jax.nn.sigmoid(g_c.astype(jnp.float32))).astype(x.dtype)
        mix = jnp.concatenate([o_a.reshape(B, L, SB_WIDTH).astype(x.dtype), o_b.astype(x.dtype), o_c], axis=-1)
        x = layer_norm(ALPHA * x + mix @ w_out[l], ln1_g[l], ln1_b[l])
        hdn = jnp.square(jax.nn.relu(x @ w_ff1[l]))
        x = layer_norm(ALPHA * x + hdn @ w_ff2[l], ln2_g[l], ln2_b[l])
        new_k.append(ka)
        new_v.append(va)
        new_s.append(S)
        new_mv.append(vn)
    mv = None if prompt else jnp.stack(new_mv)
    return x, jnp.stack(new_k), jnp.stack(new_v), jnp.stack(new_s), mv


def setup_inputs(seed: int = 0) -> dict:
    key = jax.random.key(seed)
    ks = jax.random.split(key, 20)
    nrm = lambda k, s, sc: jax.random.normal(k, s, jnp.float32) * sc
    return {
        'x_prompt': nrm(ks[0], (BATCH, SEQ, D_MODEL), 1.0),
        'x_sample': nrm(ks[1], (DEC_BATCH, DEC_SEQ, D_MODEL), 1.0),
        'cache_sb_k': nrm(ks[2], (DEPTH, DEC_BATCH, PAST_LEN, SB_HEADS, HEAD_DIM), 1.0),
        'cache_sb_v': nrm(ks[3], (DEPTH, DEC_BATCH, PAST_LEN, SB_HEADS, HEAD_DIM), 1.0),
        'state_hgrn': nrm(ks[4], (DEPTH, DEC_BATCH, HG_HEADS, HG_KDIM, HG_VDIM), 0.5),
        'w_in': nrm(ks[5], (DEPTH, D_MODEL, IN_WIDTH), D_MODEL ** -0.5),
        'w_out': nrm(ks[6], (DEPTH, MIX_WIDTH, D_MODEL), MIX_WIDTH ** -0.5 * BETA_INIT),
        'mlp_ln_g': 1.0 + nrm(ks[7], (DEPTH, MLP_WIDTH), 0.05),
        'mlp_ln_b': nrm(ks[8], (DEPTH, MLP_WIDTH), 0.02),
        'mlp_ws': nrm(ks[9], (DEPTH, MLP_GROUPS, MLP_CHUNK, MLP_CHUNK), MLP_CHUNK ** -0.5),
        'mlp_bs': 1.0 + nrm(ks[10], (DEPTH, MLP_GROUPS, MLP_CHUNK), 0.1),
        'hg_lb_logits': nrm(ks[11], (DEPTH, HG_HEADS * HG_KDIM), 0.5),
        'hg_norm_w': 1.0 + nrm(ks[12], (DEPTH, HG_WIDTH), 0.05),
        'ln1_g': 1.0 + nrm(ks[13], (DEPTH, D_MODEL), 0.05),
        'ln1_b': nrm(ks[14], (DEPTH, D_MODEL), 0.02),
        'w_ff1': nrm(ks[15], (DEPTH, D_MODEL, D_FF), D_MODEL ** -0.5),
        'w_ff2': nrm(ks[16], (DEPTH, D_FF, D_MODEL), D_FF ** -0.5 * BETA_INIT),
        'ln2_g': 1.0 + nrm(ks[17], (DEPTH, D_MODEL), 0.05),
        'ln2_b': nrm(ks[18], (DEPTH, D_MODEL), 0.02),
    }


def reference(x_prompt, x_sample, cache_sb_k, cache_sb_v, state_hgrn, w_in, w_out, mlp_ln_g, mlp_ln_b,
              mlp_ws, mlp_bs, hg_lb_logits, hg_norm_w, ln1_g, ln1_b, w_ff1, w_ff2, ln2_g, ln2_b):
    y_prompt, new_sb_k_prompt, new_sb_v_prompt, new_hgrn_prompt, _unused = trunk(
        x_prompt, None, None, None, w_in, w_out, mlp_ln_g, mlp_ln_b, mlp_ws, mlp_bs,
        hg_lb_logits, hg_norm_w, ln1_g, ln1_b, w_ff1, w_ff2, ln2_g, ln2_b)
    y_sample, new_sb_k_sample, new_sb_v_sample, new_hgrn_sample, new_mlp_v_sample = trunk(
        x_sample, cache_sb_k, cache_sb_v, state_hgrn, w_in, w_out, mlp_ln_g, mlp_ln_b, mlp_ws, mlp_bs,
        hg_lb_logits, hg_norm_w, ln1_g, ln1_b, w_ff1, w_ff2, ln2_g, ln2_b)
    return (y_prompt, y_sample, new_sb_k_prompt, new_sb_v_prompt, new_hgrn_prompt,
            new_sb_k_sample, new_sb_v_sample, new_hgrn_sample, new_mlp_v_sample)
```

```python
import functools
import math

import jax
import jax.numpy as jnp
from jax import lax
from jax.experimental import pallas as pl
from jax.experimental.pallas import tpu as pltpu

F32 = jnp.float32
BF16 = jnp.bfloat16

HEAD_DIM = 64
SB_HEADS = 6
SB_WIDTH = SB_HEADS * HEAD_DIM
MLP_GROUPS = 4
MLP_WIDTH = MLP_GROUPS * HEAD_DIM
MLP_CHUNK = 128
HG_HEADS = 6
HG_WIDTH = HG_HEADS * HEAD_DIM
HG_CHUNK = 64
HG_SUB = 16
KEY_BLOCK = 128
LN_EPS = 1e-5
RMS_EPS = 1e-6

OFF_QA = 0
OFF_KA = OFF_QA + SB_WIDTH
OFF_VA = OFF_KA + SB_WIDTH
OFF_UB = OFF_VA + SB_WIDTH
OFF_VB = OFF_UB + MLP_WIDTH
OFF_QC = OFF_VB + MLP_WIDTH
OFF_FC = OFF_QC + HG_WIDTH
OFF_IC = OFF_FC + HG_WIDTH
OFF_GC = OFF_IC + HG_WIDTH
IN_WIDTH = OFF_GC + HG_WIDTH

V7X_VMEM_BYTES = 64 * 1024 * 1024
VMEM_LIMIT_BYTES = V7X_VMEM_BYTES - 8 * 1024 * 1024


def _dot(a, b):
    return jnp.dot(a, b, preferred_element_type=F32)


def _dot_nt(a, b):
    return lax.dot_general(a, b, (((1,), (1,)), ((), ())), preferred_element_type=F32)


def _sigmoid(x):
    return 1.0 / (1.0 + jnp.exp(-x))


def _log_sigmoid(x):
    return jnp.minimum(x, 0.0) - jnp.log1p(jnp.exp(-jnp.abs(x)))


def _layer_norm(x, g, b):
    mu = jnp.mean(x, axis=-1, keepdims=True)
    xc = x - mu
    var = jnp.mean(xc * xc, axis=-1, keepdims=True)
    return xc * lax.rsqrt(var + LN_EPS) * g + b


def _split_hi_lo(x):
    hi = x.astype(BF16)
    lo = (x - hi.astype(F32)).astype(BF16)
    return hi, lo


def _head_block_ones(width):
    r = lax.broadcasted_iota(jnp.int32, (width, width), 0) // HEAD_DIM
    c = lax.broadcasted_iota(jnp.int32, (width, width), 1) // HEAD_DIM
    return jnp.where(r == c, 1.0, 0.0).astype(BF16)


def _proj_kernel(x_ref, w_ref, lng_ref, lnb_ref, wm_ref, brow_ref, loglb_ref, l1mlb_ref, omlb_ref,
                 qa_ref, ka_ref, va_ref, kab_ref, vab_ref, ob_ref, qs_ref, kk_ref, hv_ref, lf_ref,
                 gt_ref, *maybe_vn_ref, tm, cl):
    xb = x_ref[...].astype(BF16)

    def proj(off, width):
        return _dot(xb, w_ref[:, off:off + width])

    qa_ref[...] = (proj(OFF_QA, SB_WIDTH) * (1.0 / math.sqrt(HEAD_DIM))).astype(BF16)
    ka = proj(OFF_KA, SB_WIDTH)
    ka_ref[...] = ka
    kab_ref[...] = ka.astype(BF16)
    va = proj(OFF_VA, SB_WIDTH)
    va_ref[...] = va
    vab_ref[...] = va.astype(BF16)

    vn = _layer_norm(proj(OFF_VB, MLP_WIDTH), lng_ref[...], lnb_ref[...])
    if maybe_vn_ref:
        maybe_vn_ref[0][...] = vn
    ub = proj(OFF_UB, MLP_WIDTH)
    row = lax.broadcasted_iota(jnp.int32, (MLP_CHUNK, MLP_CHUNK), 0)
    col = lax.broadcasted_iota(jnp.int32, (MLP_CHUNK, MLP_CHUNK), 1)
    keep = (col <= row) & ((row // cl) == (col // cl))
    wm = [jnp.where(keep, wm_ref[g], 0.0).astype(BF16) for g in range(MLP_GROUPS)]
    left_head = lax.broadcasted_iota(jnp.int32, (MLP_CHUNK, 2 * HEAD_DIM), 1) < HEAD_DIM
    brow = brow_ref[...]
    vnb = vn.astype(BF16)
    for c in range(tm // MLP_CHUNK):
        rows = slice(c * MLP_CHUNK, (c + 1) * MLP_CHUNK)
        for p in range(MLP_GROUPS // 2):
            lanes = slice(p * 2 * HEAD_DIM, (p + 1) * 2 * HEAD_DIM)
            vp = vnb[rows, lanes]
            mixed = jnp.where(left_head, _dot(wm[2 * p], vp), _dot(wm[2 * p + 1], vp)) + brow[:, lanes]
            ob_ref[rows, lanes] = (ub[rows, lanes] * mixed).astype(BF16)

    qc = proj(OFF_QC, HG_WIDTH)
    qs_ref[...] = qc * _sigmoid(qc)
    zf = proj(OFF_FC, HG_WIDTH)
    c_term = l1mlb_ref[...] + _log_sigmoid(zf)
    a_term = loglb_ref[...]
    lf_ref[...] = jnp.maximum(a_term, c_term) + jnp.log1p(jnp.exp(-jnp.abs(a_term - c_term)))
    kk_ref[...] = omlb_ref[...] * _sigmoid(-zf)
    hv_ref[...] = proj(OFF_IC, HG_WIDTH)
    gt_ref[...] = _sigmoid(proj(OFF_GC, HG_WIDTH))


def _proj_call(x2d, w_in_bf, ln_g, ln_b, wm_tiled, brow, loglb, l1mlb, omlb, *, cl, emit_vn):
    n, d = x2d.shape
    tm = min(512, n)
    assert n % tm == 0 and tm % MLP_CHUNK == 0 and MLP_CHUNK % cl == 0
    row_spec = lambda width: pl.BlockSpec((tm, width), lambda i: (i, 0))
    const = lambda shape: pl.BlockSpec(shape, lambda i: (0,) * len(shape), pipeline_mode=pl.Buffered(1))
    out_shapes = [
        jax.ShapeDtypeStruct((n, SB_WIDTH), BF16),
        jax.ShapeDtypeStruct((n, SB_WIDTH), F32),
        jax.ShapeDtypeStruct((n, SB_WIDTH), F32),
        jax.ShapeDtypeStruct((n, SB_WIDTH), BF16),
        jax.ShapeDtypeStruct((n, SB_WIDTH), BF16),
        jax.ShapeDtypeStruct((n, MLP_WIDTH), BF16),
        jax.ShapeDtypeStruct((n, HG_WIDTH), F32),
        jax.ShapeDtypeStruct((n, HG_WIDTH), F32),
        jax.ShapeDtypeStruct((n, HG_WIDTH), F32),
        jax.ShapeDtypeStruct((n, HG_WIDTH), F32),
        jax.ShapeDtypeStruct((n, HG_WIDTH), F32),
    ]
    if emit_vn:
        out_shapes.append(jax.ShapeDtypeStruct((n, MLP_WIDTH), F32))
    out_specs = [row_spec(s.shape[1]) for s in out_shapes]
    return pl.pallas_call(
        functools.partial(_proj_kernel, tm=tm, cl=cl),
        grid=(n // tm,),
        in_specs=[
            row_spec(d),
            const((d, IN_WIDTH)),
            const((1, MLP_WIDTH)), const((1, MLP_WIDTH)),
            const((MLP_GROUPS, MLP_CHUNK, MLP_CHUNK)),
            const((MLP_CHUNK, MLP_WIDTH)),
            const((1, HG_WIDTH)), const((1, HG_WIDTH)), const((1, HG_WIDTH)),
        ],
        out_specs=out_specs,
        out_shape=out_shapes,
        compiler_params=pltpu.CompilerParams(dimension_semantics=("parallel",),
                                             vmem_limit_bytes=VMEM_LIMIT_BYTES),
        name="proj",
    )(x2d, w_in_bf, ln_g, ln_b, wm_tiled, brow, loglb, l1mlb, omlb)


def _attn_kernel(q_ref, k_ref, v_ref, o_ref, *, tq, nq, q_start):
    kb = KEY_BLOCK
    r = lax.broadcasted_iota(jnp.int32, (2 * kb, 2 * kb), 0) % kb
    c = lax.broadcasted_iota(jnp.int32, (2 * kb, 2 * kb), 1)
    uu = jnp.where((c >= kb) | (r > c), 1.0, 0.0).astype(BF16)
    row = lax.broadcasted_iota(jnp.int32, (tq, kb), 0)
    col = lax.broadcasted_iota(jnp.int32, (tq, kb), 1)

    def block(q, k0, mask, run, acc, lanes):
        k = k_ref[0, pl.ds(k0, kb), lanes]
        v = v_ref[0, pl.ds(k0, kb), lanes]
        z = _dot_nt(q, k)
        log_beta = _log_sigmoid(z)
        log_keep = log_beta - z
        if mask is not None:
            log_keep = jnp.where(mask, log_keep, 0.0)
        hi, lo = _split_hi_lo(log_keep)
        sums = _dot(jnp.concatenate([hi, lo], axis=1), uu)
        w = jnp.exp(log_beta + sums[:, :kb] + run)
        if mask is not None:
            w = jnp.where(mask, w, 0.0)
        return run + sums[:, kb:], acc + _dot(w.astype(BF16), v)

    def q_block(qi, carry):
        r0 = pl.multiple_of(qi * tq, tq)
        t0 = q_start + qi * tq
        kd = t0 // kb
        mask = (kd * kb + col) < (t0 + row)
        k0d = pl.multiple_of(kd * kb, kb)
        state = []
        qs = []
        for h in range(2):
            lanes = slice(h * HEAD_DIM, (h + 1) * HEAD_DIM)
            q = q_ref[0, pl.ds(r0, tq), lanes]
            qs.append(q)
            run0 = jnp.zeros((tq, kb), F32)
            acc0 = jnp.zeros((tq, HEAD_DIM), F32)
            state.extend(block(q, k0d, mask, run0, acc0, lanes))

        def k_block(i, st):
            k0 = pl.multiple_of((kd - 1 - i) * kb, kb)
            out = []
            for h in range(2):
                lanes = slice(h * HEAD_DIM, (h + 1) * HEAD_DIM)
                out.extend(block(qs[h], k0, None, st[2 * h], st[2 * h + 1], lanes))
            return tuple(out)

        state = lax.fori_loop(0, kd, k_block, tuple(state))
        o_ref[0, pl.ds(r0, tq), :] = jnp.concatenate([state[1], state[3]], axis=1).astype(BF16)
        return carry

    lax.fori_loop(0, nq, q_block, 0)


def _attn_call(q, k, v, *, q_start, tq):
    b, lq, _ = q.shape
    lk = k.shape[1]
    assert lq % tq == 0 and KEY_BLOCK % tq == 0 and q_start % tq == 0 and lk % KEY_BLOCK == 0
    assert q_start + lq <= lk
    nq = lq // tq
    spec = lambda l: pl.BlockSpec((1, l, 2 * HEAD_DIM), lambda i, p: (i, 0, p))
    return pl.pallas_call(
        functools.partial(_attn_kernel, tq=tq, nq=nq, q_start=q_start),
        grid=(b, SB_HEADS // 2),
        in_specs=[spec(lq), spec(lk), spec(lk)],
        out_specs=spec(lq),
        out_shape=jax.ShapeDtypeStruct((b, lq, SB_WIDTH), BF16),
        compiler_params=pltpu.CompilerParams(dimension_semantics=("parallel", "parallel"),
                                             vmem_limit_bytes=VMEM_LIMIT_BYTES),
        name="attn",
    )(q, k, v)


def _hgrn_kernel(qs_ref, kk_ref, hv_ref, lf_ref, gt_ref, nw_ref, s0_ref, o_ref, st_ref, *, rb, t):
    nsb = t // HG_SUB
    sub = HG_SUB
    w = HG_WIDTH

    @pl.when(pl.program_id(1) == 0)
    def _():
        st_ref[...] = s0_ref[...]

    head_ones = _head_block_ones(w)
    tri_incl = jnp.where(lax.broadcasted_iota(jnp.int32, (t, t), 1) <= lax.broadcasted_iota(jnp.int32, (t, t), 0),
                         1.0, 0.0).astype(BF16)
    sub_row = lax.broadcasted_iota(jnp.int32, (nsb, sub, w), 1)
    if nsb > 1:
        n_stack = sub * (nsb * (nsb - 1) // 2)
        seg_start = [sub * (i * (i - 1) // 2) for i in range(nsb + 1)]
        tr = lax.broadcasted_iota(jnp.int32, (t, n_stack), 0) // sub
        tc = lax.broadcasted_iota(jnp.int32, (t, n_stack), 1)
        pair = jnp.zeros((t, n_stack), jnp.bool_)
        for i in range(1, nsb):
            pair = pair | ((tr == i) & (tc >= seg_start[i]) & (tc < seg_start[i + 1]))

    def chunk(ci, carry):
        r0 = pl.multiple_of(ci * t, t)
        rows = pl.ds(r0, t)
        q = qs_ref[0, rows, :]
        k = kk_ref[0, rows, :]
        v = hv_ref[0, rows, :]
        lf = lf_ref[0, rows, :]
        lf_hi, lf_lo = _split_hi_lo(lf)
        b = _dot(tri_incl, lf_hi) + _dot(tri_incl, lf_lo)
        b_last = b[t - 1:t, :]
        q_in = (q * jnp.exp(b)).astype(BF16)
        k_st = (k * jnp.exp(b_last - b)).astype(BF16)
        decay = jnp.exp(b_last)
        vb = v.astype(BF16)

        b3 = b.reshape(nsb, sub, w)
        q3 = q.reshape(nsb, sub, w)
        k3 = k.reshape(nsb, sub, w)
        v3 = v.reshape(nsb, sub, w)
        terms = []
        for s in range(sub):
            bs = jnp.broadcast_to(b3[:, s:s + 1, :], (nsb, sub, w))
            ks = jnp.broadcast_to(k3[:, s:s + 1, :], (nsb, sub, w))
            e = jnp.where(sub_row >= s, jnp.exp(b3 - bs), 0.0)
            terms.append((q3 * ks * e).reshape(nsb * sub, w))
        a_all = jnp.concatenate(terms, axis=0).astype(BF16)
        p_all = _dot(a_all, head_ones)
        o = jnp.zeros((nsb, sub, w), F32)
        for s in range(sub):
            vs = jnp.broadcast_to(v3[:, s:s + 1, :], (nsb, sub, w))
            o = o + p_all[s * t:(s + 1) * t, :].reshape(nsb, sub, w) * vs
        o = o.reshape(t, w)

        if nsb > 1:
            qt_parts = [jnp.zeros((sub, w), F32)]
            kt_parts = []
            vs_parts = []
            for i in range(1, nsb):
                ref_row = b[i * sub - 1:i * sub, :]
                qt_parts.append(q[i * sub:(i + 1) * sub, :] * jnp.exp(b[i * sub:(i + 1) * sub, :] - ref_row))
                kt_parts.append(k[:i * sub, :] * jnp.exp(ref_row - b[:i * sub, :]))
                vs_parts.append(v[:i * sub, :])
            qt = jnp.concatenate(qt_parts, axis=0).astype(BF16)
            kt = jnp.concatenate(kt_parts, axis=0).astype(BF16)
            vst = jnp.concatenate(vs_parts, axis=0).astype(BF16)

        o_heads = []
        for h in range(HG_HEADS):
            lanes = slice(h * HEAD_DIM, (h + 1) * HEAD_DIM)
            s_t = st_ref[0, h]
            oh = _dot_nt(q_in[:, lanes], s_t.astype(BF16))
            if nsb > 1:
                sc = jnp.where(pair, _dot_nt(qt[:, lanes], kt[:, lanes]), 0.0)
                oh = oh + _dot(sc.astype(BF16), vst[:, lanes])
            o_heads.append(oh)
            v_t = jnp.transpose(v[:, lanes]).astype(BF16)
            st_ref[0, h] = s_t * decay[:, lanes] + _dot(v_t, k_st[:, lanes])
        o = o + jnp.concatenate(o_heads, axis=1)

        sq_hi, sq_lo = _split_hi_lo(o * o)
        ms = (_dot(sq_hi, head_ones) + _dot(sq_lo, head_ones)) * (1.0 / HEAD_DIM)
        o_ref[0, rows, :] = (o * lax.rsqrt(ms + RMS_EPS) * nw_ref[...] * gt_ref[0, rows, :]).astype(BF16)
        return carry

    lax.fori_loop(0, rb // t, chunk, 0)


def _hgrn_call(qs, kk, hv, lf, gt, norm_w, s0_t, *, t):
    b, l, w = qs.shape
    rb = min(512, l)
    assert l % rb == 0 and rb % t == 0 and t % HG_SUB == 0
    row_spec = pl.BlockSpec((1, rb, w), lambda i, j: (i, j, 0))
    st_spec = pl.BlockSpec((1, HG_HEADS, HEAD_DIM, HEAD_DIM), lambda i, j: (i, 0, 0, 0))
    return pl.pallas_call(
        functools.partial(_hgrn_kernel, rb=rb, t=t),
        grid=(b, l // rb),
        in_specs=[row_spec] * 5 + [pl.BlockSpec((1, w), lambda i, j: (0, 0)), st_spec],
        out_specs=[row_spec, st_spec],
        out_shape=[jax.ShapeDtypeStruct((b, l, w), BF16),
                   jax.ShapeDtypeStruct((b, HG_HEADS, HEAD_DIM, HEAD_DIM), F32)],
        compiler_params=pltpu.CompilerParams(dimension_semantics=("parallel", "arbitrary"),
                                             vmem_limit_bytes=VMEM_LIMIT_BYTES),
        name="hgrn",
    )(qs, kk, hv, lf, gt, norm_w, s0_t)


def _ffn_kernel(x_ref, oa_ref, ob_ref, oc_ref, wo_ref, g1_ref, b1_ref, w1_ref, w2_ref, g2_ref, b2_ref,
                y_ref, *, alpha, ff_chunk):
    mix = (_dot(oa_ref[...], wo_ref[0:SB_WIDTH, :])
           + _dot(ob_ref[...], wo_ref[SB_WIDTH:SB_WIDTH + MLP_WIDTH, :])
           + _dot(oc_ref[...], wo_ref[SB_WIDTH + MLP_WIDTH:, :]))
    x1 = _layer_norm(alpha * x_ref[...] + mix, g1_ref[...], b1_ref[...])
    x1b = x1.astype(BF16)
    d_ff = w1_ref.shape[1]
    ff = jnp.zeros_like(x1)
    for c in range(d_ff // ff_chunk):
        cols = slice(c * ff_chunk, (c + 1) * ff_chunk)
        hdn = jnp.maximum(_dot(x1b, w1_ref[:, cols]), 0.0)
        ff = ff + _dot((hdn * hdn).astype(BF16), w2_ref[cols, :])
    y_ref[...] = _layer_norm(alpha * x1 + ff, g2_ref[...], b2_ref[...])


def _ffn_call(x2d, oa, ob, oc, wo, g1, b1, w1, w2, g2, b2, *, alpha):
    n, d = x2d.shape
    d_ff = w1.shape[1]
    tm = min(512, n)
    ff_chunk = min(1024, d_ff)
    assert n % tm == 0 and d_ff % ff_chunk == 0
    row_spec = lambda width: pl.BlockSpec((tm, width), lambda i: (i, 0))
    const = lambda shape: pl.BlockSpec(shape, lambda i: (0,) * len(shape), pipeline_mode=pl.Buffered(1))
    return pl.pallas_call(
        functools.partial(_ffn_kernel, alpha=alpha, ff_chunk=ff_chunk),
        grid=(n // tm,),
        in_specs=[row_spec(d), row_spec(SB_WIDTH), row_spec(MLP_WIDTH), row_spec(HG_WIDTH),
                  const(wo.shape), const((1, d)), const((1, d)),
                  const(w1.shape), const(w2.shape), const((1, d)), const((1, d))],
        out_specs=row_spec(d),
        out_shape=jax.ShapeDtypeStruct((n, d), F32),
        compiler_params=pltpu.CompilerParams(dimension_semantics=("parallel",),
                                             vmem_limit_bytes=VMEM_LIMIT_BYTES),
        name="ffn",
    )(x2d, oa, ob, oc, wo, g1, b1, w1, w2, g2, b2)


def _trunk(x, past_k, past_v, hg_state, w_in, w_out, mlp_ln_g, mlp_ln_b, mlp_ws, mlp_bs,
           hg_lb_logits, hg_norm_w, ln1_g, ln1_b, w_ff1, w_ff2, ln2_g, ln2_b):
    bsz, l, d = x.shape
    depth = w_in.shape[0]
    n = bsz * l
    alpha = (2 * depth) ** 0.25
    prompt = past_k is None
    cl = MLP_CHUNK if prompt else l
    assert MLP_CHUNK % cl == 0 and l % cl == 0
    reps = MLP_CHUNK // cl

    g = jax.nn.softmax(hg_lb_logits.astype(F32), axis=0)
    cs = jnp.cumsum(g, axis=0)
    lbs = cs - cs[0:1]

    x2d = x.reshape(n, d)
    new_k, new_v, new_s, new_mv = [], [], [], []
    for layer in range(depth):
        lb = lbs[layer][None, :]
        wm_tiled = jnp.tile(mlp_ws[layer][:, :cl, :cl], (1, reps, reps))
        brow = jnp.tile(jnp.repeat(jnp.transpose(mlp_bs[layer][:, :cl]), HEAD_DIM, axis=1), (reps, 1))
        outs = _proj_call(x2d, w_in[layer].astype(BF16), mlp_ln_g[layer][None], mlp_ln_b[layer][None],
                          wm_tiled, brow, jnp.log(lb), jnp.log1p(-lb), 1.0 - lb, cl=cl, emit_vn=not prompt)
        qa, ka, va, kab, vab, ob, qs, kk, hv, lf, gt = outs[:11]
        shp = lambda a: a.reshape(bsz, l, a.shape[-1])
        if prompt:
            oa = _attn_call(shp(qa), shp(kab), shp(vab), q_start=0, tq=KEY_BLOCK)
            s0_t = jnp.zeros((bsz, HG_HEADS, HEAD_DIM, HEAD_DIM), F32)
            t = HG_CHUNK
        else:
            past_len = past_k.shape[2]
            pad = (-(past_len + l)) % KEY_BLOCK
            cat = lambda past, new: jnp.pad(
                jnp.concatenate([past.reshape(bsz, past_len, SB_WIDTH).astype(BF16), shp(new)], axis=1),
                ((0, 0), (0, pad), (0, 0)))
            oa = _attn_call(shp(qa), cat(past_k[layer], kab), cat(past_v[layer], vab), q_start=past_len, tq=l)
            s0_t = jnp.swapaxes(hg_state[layer].astype(F32), -1, -2)
            t = l
            new_mv.append(outs[11].reshape(bsz, l, MLP_WIDTH))
        oc, s_t = _hgrn_call(shp(qs), shp(kk), shp(hv), shp(lf), shp(gt), hg_norm_w[layer][None], s0_t, t=t)
        x2d = _ffn_call(x2d, oa.reshape(n, SB_WIDTH), ob, oc.reshape(n, HG_WIDTH),
                        w_out[layer].astype(BF16), ln1_g[layer][None], ln1_b[layer][None],
                        w_ff1[layer].astype(BF16), w_ff2[layer].astype(BF16),
                        ln2_g[layer][None], ln2_b[layer][None], alpha=alpha)
        new_k.append(ka.reshape(bsz, l, SB_HEADS, HEAD_DIM))
        new_v.append(va.reshape(bsz, l, SB_HEADS, HEAD_DIM))
        new_s.append(jnp.swapaxes(s_t, -1, -2))
    mv = None if prompt else jnp.stack(new_mv)
    return x2d.reshape(bsz, l, d), jnp.stack(new_k), jnp.stack(new_v), jnp.stack(new_s), mv


def kernel(x_prompt, x_sample, cache_sb_k, cache_sb_v, state_hgrn, w_in, w_out, mlp_ln_g, mlp_ln_b, mlp_ws, mlp_bs, hg_lb_logits, hg_norm_w, ln1_g, ln1_b, w_ff1, w_ff2, ln2_g, ln2_b):
    weights = (w_in, w_out, mlp_ln_g, mlp_ln_b, mlp_ws, mlp_bs, hg_lb_logits, hg_norm_w,
               ln1_g, ln1_b, w_ff1, w_ff2, ln2_g, ln2_b)
    y_p, k_p, v_p, s_p, _ = _trunk(x_prompt, None, None, None, *weights)
    y_s, k_s, v_s, s_s, mv_s = _trunk(x_sample, cache_sb_k, cache_sb_v, state_hgrn, *weights)
    return (y_p, y_s, k_p, v_p, s_p, k_s, v_s, s_s, mv_s)
```

```python
import functools
import math

import jax
import jax.numpy as jnp
from jax import lax
from jax.experimental import pallas as pl
from jax.experimental.pallas import tpu as pltpu

F32 = jnp.float32
BF16 = jnp.bfloat16

HEAD_DIM = 64
SB_HEADS = 6
SB_WIDTH = SB_HEADS * HEAD_DIM
MLP_GROUPS = 4
MLP_WIDTH = MLP_GROUPS * HEAD_DIM
MLP_CHUNK = 128
HG_HEADS = 6
HG_WIDTH = HG_HEADS * HEAD_DIM
HG_CHUNK = 64
HG_SUB = 8
HG_GROUP = 4
MXU_WIDTH = 256
KEY_BLOCK = 128
SUPER_KEYS = 512
LOG2E = 1.4426950408889634
LN_EPS = 1e-5
RMS_EPS = 1e-6

OFF_QA = 0
OFF_KA = OFF_QA + SB_WIDTH
OFF_VA = OFF_KA + SB_WIDTH
OFF_UB = OFF_VA + SB_WIDTH
OFF_VB = OFF_UB + MLP_WIDTH
OFF_QC = OFF_VB + MLP_WIDTH
OFF_FC = OFF_QC + HG_WIDTH
OFF_IC = OFF_FC + HG_WIDTH
OFF_GC = OFF_IC + HG_WIDTH
IN_WIDTH = OFF_GC + HG_WIDTH

V7X_VMEM_BYTES = 64 * 1024 * 1024
VMEM_LIMIT_BYTES = V7X_VMEM_BYTES - 8 * 1024 * 1024


def _dot(a, b):
    return jnp.dot(a, b, preferred_element_type=F32)


def _dot_nt(a, b):
    return lax.dot_general(a, b, (((1,), (1,)), ((), ())), preferred_element_type=F32)


def _sigmoid(x):
    return 1.0 / (1.0 + jnp.exp(-x))


def _log_sigmoid(x):
    return jnp.minimum(x, 0.0) - jnp.log1p(jnp.exp(-jnp.abs(x)))


def _layer_norm(x, g, b):
    mu = jnp.mean(x, axis=-1, keepdims=True)
    xc = x - mu
    var = jnp.mean(xc * xc, axis=-1, keepdims=True)
    return xc * lax.rsqrt(var + LN_EPS) * g + b


def _split_hi_lo(x):
    hi = x.astype(BF16)
    lo = (x - hi.astype(F32)).astype(BF16)
    return hi, lo


def _head_block_ones(width):
    r = lax.broadcasted_iota(jnp.int32, (width, width), 0) // HEAD_DIM
    c = lax.broadcasted_iota(jnp.int32, (width, width), 1) // HEAD_DIM
    return jnp.where(r == c, 1.0, 0.0).astype(BF16)


def _head_sums(x, ones_wide, ones_narrow):
    split = ones_wide.shape[0]
    return jnp.concatenate([_dot(x[:, :split], ones_wide), _dot(x[:, split:], ones_narrow)], axis=1)


def _proj_kernel(x_ref, w_ref, lng_ref, lnb_ref, wm_ref, brow_ref, loglb_ref, l1mlb_ref, omlb_ref,
                 *rest, tm, cl, n_carried):
    (ka_ref, va_ref, qa_ref, kab_ref, vab_ref, ob_ref, qs_ref, kk_ref, hv_ref, lf_ref,
     gt_ref, *maybe_vn_ref) = rest[n_carried:]
    xb = x_ref[...].astype(BF16)

    def proj(off, width):
        return _dot(xb, w_ref[:, off:off + width])

    qa_ref[...] = (proj(OFF_QA, SB_WIDTH) * (LOG2E / math.sqrt(HEAD_DIM))).astype(BF16)
    ka = proj(OFF_KA, SB_WIDTH)
    ka_ref[...] = ka
    kab_ref[...] = ka.astype(BF16)
    va = proj(OFF_VA, SB_WIDTH)
    va_ref[...] = va
    vab_ref[...] = va.astype(BF16)

    vn = _layer_norm(proj(OFF_VB, MLP_WIDTH), lng_ref[...], lnb_ref[...])
    if maybe_vn_ref:
        maybe_vn_ref[0][...] = vn
    ub = proj(OFF_UB, MLP_WIDTH)
    row = lax.broadcasted_iota(jnp.int32, (MLP_CHUNK, MLP_CHUNK), 0)
    col = lax.broadcasted_iota(jnp.int32, (MLP_CHUNK, MLP_CHUNK), 1)
    keep = (col <= row) & ((row // cl) == (col // cl))
    wm = [jnp.where(keep, wm_ref[g], 0.0).astype(BF16) for g in range(MLP_GROUPS)]
    left_head = lax.broadcasted_iota(jnp.int32, (MLP_CHUNK, 2 * HEAD_DIM), 1) < HEAD_DIM
    brow = brow_ref[...]
    vnb = vn.astype(BF16)
    for c in range(tm // MLP_CHUNK):
        rows = slice(c * MLP_CHUNK, (c + 1) * MLP_CHUNK)
        for p in range(MLP_GROUPS // 2):
            lanes = slice(p * 2 * HEAD_DIM, (p + 1) * 2 * HEAD_DIM)
            vp = vnb[rows, lanes]
            mixed = jnp.where(left_head, _dot(wm[2 * p], vp), _dot(wm[2 * p + 1], vp)) + brow[:, lanes]
            ob_ref[rows, lanes] = (ub[rows, lanes] * mixed).astype(BF16)

    qc = proj(OFF_QC, HG_WIDTH)
    qs_ref[...] = qc * _sigmoid(qc)
    zf = proj(OFF_FC, HG_WIDTH)
    c_term = l1mlb_ref[...] + _log_sigmoid(zf)
    a_term = loglb_ref[...]
    lf_ref[...] = jnp.maximum(a_term, c_term) + jnp.log1p(jnp.exp(-jnp.abs(a_term - c_term)))
    kk_ref[...] = omlb_ref[...] * _sigmoid(-zf)
    hv_ref[...] = proj(OFF_IC, HG_WIDTH)
    gt_ref[...] = _sigmoid(proj(OFF_GC, HG_WIDTH))


def _proj_call(x2d, w_in_bf, ln_g, ln_b, wm_tiled, brow, loglb, l1mlb, omlb, carried_kv, *,
               layer, depth, cl, emit_vn):
    n, d = x2d.shape
    tm = min(512, n)
    assert n % tm == 0 and tm % MLP_CHUNK == 0 and MLP_CHUNK % cl == 0
    row_spec = lambda width: pl.BlockSpec((tm, width), lambda i: (i, 0))
    layer_spec = pl.BlockSpec((None, tm, SB_WIDTH), lambda i: (layer, i, 0))
    const = lambda shape: pl.BlockSpec(shape, lambda i: (0,) * len(shape), pipeline_mode=pl.Buffered(1))
    out_shapes = [
        jax.ShapeDtypeStruct((depth, n, SB_WIDTH), F32),
        jax.ShapeDtypeStruct((depth, n, SB_WIDTH), F32),
        jax.ShapeDtypeStruct((n, SB_WIDTH), BF16),
        jax.ShapeDtypeStruct((n, SB_WIDTH), BF16),
        jax.ShapeDtypeStruct((n, SB_WIDTH), BF16),
        jax.ShapeDtypeStruct((n, MLP_WIDTH), BF16),
        jax.ShapeDtypeStruct((n, HG_WIDTH), F32),
        jax.ShapeDtypeStruct((n, HG_WIDTH), F32),
        jax.ShapeDtypeStruct((n, HG_WIDTH), F32),
        jax.ShapeDtypeStruct((n, HG_WIDTH), F32),
        jax.ShapeDtypeStruct((n, HG_WIDTH), F32),
    ]
    if emit_vn:
        out_shapes.append(jax.ShapeDtypeStruct((n, MLP_WIDTH), F32))
    out_specs = [layer_spec, layer_spec] + [row_spec(s.shape[1]) for s in out_shapes[2:]]
    n_fixed = 9
    return pl.pallas_call(
        functools.partial(_proj_kernel, tm=tm, cl=cl, n_carried=len(carried_kv)),
        grid=(n // tm,),
        in_specs=[
            row_spec(d),
            const((d, IN_WIDTH)),
            const((1, MLP_WIDTH)), const((1, MLP_WIDTH)),
            const((MLP_GROUPS, MLP_CHUNK, MLP_CHUNK)),
            const((MLP_CHUNK, MLP_WIDTH)),
            const((1, HG_WIDTH)), const((1, HG_WIDTH)), const((1, HG_WIDTH)),
        ] + [pl.BlockSpec(memory_space=pl.ANY)] * len(carried_kv),
        out_specs=out_specs,
        out_shape=out_shapes,
        input_output_aliases={n_fixed + j: j for j in range(len(carried_kv))},
        compiler_params=pltpu.CompilerParams(dimension_semantics=("parallel",),
                                             vmem_limit_bytes=VMEM_LIMIT_BYTES),
        name="proj",
    )(x2d, w_in_bf, ln_g, ln_b, wm_tiled, brow, loglb, l1mlb, omlb, *carried_kv)


def _sb_scores(z, mask, uu):
    kb = KEY_BLOCK
    log_beta = jnp.minimum(z, 0.0) - jnp.log2(1.0 + jnp.exp2(-jnp.abs(z)))
    log_keep = log_beta - z
    if mask is not None:
        log_keep = jnp.where(mask, log_keep, 0.0)
    hi, lo = _split_hi_lo(log_keep)
    sums = _dot(jnp.concatenate([hi, lo], axis=1), uu)
    return log_beta + sums[:, :kb], sums[:, kb:]


def _sb_finish(pre, total, mask, run):
    w = jnp.exp2(pre + run)
    if mask is not None:
        w = jnp.where(mask, w, 0.0)
    return w.astype(BF16), run + total


def _attn_kernel(q_ref, k_ref, v_ref, o_ref, run_ref, acc_ref, *, tq, nstrip, nsuper, q_start):
    kb = KEY_BLOCK
    r = lax.broadcasted_iota(jnp.int32, (2 * kb, 2 * kb), 0) % kb
    c = lax.broadcasted_iota(jnp.int32, (2 * kb, 2 * kb), 1)
    uu = jnp.where((c >= kb) | (r > c), 1.0, 0.0).astype(BF16)
    n_rows = nstrip * tq
    row = lax.broadcasted_iota(jnp.int32, (n_rows, kb), 0)
    col = lax.broadcasted_iota(jnp.int32, (n_rows, kb), 1)
    own_mask = (row >= tq) | (col < row)
    heads = [slice(h * HEAD_DIM, (h + 1) * HEAD_DIM) for h in range(2)]
    n_before = q_start // SUPER_KEYS

    def q_super(qi, carry):
        q0 = pl.multiple_of(qi * n_rows, n_rows)
        kd0 = pl.multiple_of((n_before + qi) * SUPER_KEYS, SUPER_KEYS)
        run_ref[...] = jnp.zeros_like(run_ref)
        acc_ref[...] = jnp.zeros_like(acc_ref)

        def own_scores(g):
            lo = g * tq
            out = []
            for lanes in heads:
                q = q_ref[0, pl.ds(q0 + lo, n_rows - lo), lanes]
                k = k_ref[0, pl.ds(kd0 + g * kb, kb), lanes]
                out.append(_sb_scores(_dot_nt(q, k), own_mask[:n_rows - lo], uu))
            return out

        def own_finish(g, scores):
            lo = g * tq
            for h, lanes in enumerate(heads):
                pre, total = scores[h]
                w, run = _sb_finish(pre, total, own_mask[:n_rows - lo], run_ref[h, lo:, :])
                run_ref[h, lo:, :] = run
                acc_ref[h, lo:, :] += _dot(w, v_ref[0, pl.ds(kd0 + g * kb, kb), lanes])

        pending = None
        for g in reversed(range(nstrip)):
            scores = own_scores(g)
            if pending is not None:
                own_finish(g + 1, pending)
            pending = scores
        own_finish(0, pending)

        def k_super(i, c2):
            k0 = pl.multiple_of((n_before + qi - 1 - i) * SUPER_KEYS, SUPER_KEYS)
            nblk = SUPER_KEYS // kb
            for h, lanes in enumerate(heads):
                q = q_ref[0, pl.ds(q0, n_rows), lanes]
                k = k_ref[0, pl.ds(k0, SUPER_KEYS), lanes]
                v = v_ref[0, pl.ds(k0, SUPER_KEYS), lanes]
                z = _dot_nt(q, k)
                run = run_ref[h]
                ws = [None] * nblk
                for g in reversed(range(nblk)):
                    pre, total = _sb_scores(z[:, g * kb:(g + 1) * kb], None, uu)
                    ws[g], run = _sb_finish(pre, total, None, run)
                run_ref[h] = run
                acc_ref[h] += _dot(jnp.concatenate(ws, axis=1), v)
            return c2

        lax.fori_loop(0, n_before + qi, k_super, 0)
        o_ref[0, pl.ds(q0, n_rows), :] = jnp.concatenate([acc_ref[0], acc_ref[1]], axis=1).astype(BF16)
        return carry

    lax.fori_loop(0, nsuper, q_super, 0)


def _attn_call(q, k, v, *, q_start):
    b, lq, _ = q.shape
    lk = k.shape[1]
    if lq % SUPER_KEYS == 0:
        tq, nstrip, nsuper = KEY_BLOCK, SUPER_KEYS // KEY_BLOCK, lq // SUPER_KEYS
    else:
        tq, nstrip, nsuper = lq, 1, 1
    assert tq <= KEY_BLOCK and q_start % SUPER_KEYS == 0 and q_start + lq <= lk and lk % KEY_BLOCK == 0
    spec = lambda l: pl.BlockSpec((1, l, 2 * HEAD_DIM), lambda i, p: (i, 0, p))
    return pl.pallas_call(
        functools.partial(_attn_kernel, tq=tq, nstrip=nstrip, nsuper=nsuper, q_start=q_start),
        grid=(b, SB_HEADS // 2),
        in_specs=[spec(lq), spec(lk), spec(lk)],
        out_specs=spec(lq),
        out_shape=jax.ShapeDtypeStruct((b, lq, SB_WIDTH), BF16),
        scratch_shapes=[pltpu.VMEM((2, nstrip * tq, KEY_BLOCK), F32),
                        pltpu.VMEM((2, nstrip * tq, HEAD_DIM), F32)],
        compiler_params=pltpu.CompilerParams(dimension_semantics=("parallel", "parallel"),
                                             vmem_limit_bytes=VMEM_LIMIT_BYTES),
        name="attn",
    )(q, k, v)


def _hgrn_kernel(qs_ref, kk_ref, hv_ref, lf_ref, gt_ref, nw_ref, s0_ref, o_ref, st_ref, *, rb, t, n_group):
    nsb = t // HG_SUB
    sub = HG_SUB
    w = HG_WIDTH
    heads = [slice(h * HEAD_DIM, (h + 1) * HEAD_DIM) for h in range(HG_HEADS)]

    @pl.when(pl.program_id(1) == 0)
    def _():
        st_ref[...] = s0_ref[...]

    ones_wide = _head_block_ones(MXU_WIDTH)
    ones_narrow = _head_block_ones(w - MXU_WIDTH)
    tri_incl = jnp.where(lax.broadcasted_iota(jnp.int32, (t, t), 1) <= lax.broadcasted_iota(jnp.int32, (t, t), 0),
                         1.0, 0.0).astype(BF16)
    sub_row = lax.broadcasted_iota(jnp.int32, (nsb, sub, w), 1)
    if nsb > 1:
        n_stack = sub * (nsb * (nsb - 1) // 2)
        seg_start = [sub * (i * (i - 1) // 2) for i in range(nsb + 1)]
        tr = lax.broadcasted_iota(jnp.int32, (t, n_stack), 0) // sub
        tc = lax.broadcasted_iota(jnp.int32, (t, n_stack), 1)
        pair = jnp.zeros((t, n_stack), jnp.bool_)
        for i in range(1, nsb):
            pair = pair | ((tr == i) & (tc >= seg_start[i]) & (tc < seg_start[i + 1]))

    def group(gi, carry):
        chunks = range(n_group)
        rows = [pl.ds(pl.multiple_of((gi * n_group + j) * t, t), t) for j in chunks]
        q = [qs_ref[0, r, :] for r in rows]
        k = [kk_ref[0, r, :] for r in rows]
        v = [hv_ref[0, r, :] for r in rows]

        b = []
        for r in rows:
            lf_hi, lf_lo = _split_hi_lo(lf_ref[0, r, :])
            b.append(_dot(tri_incl, lf_hi) + _dot(tri_incl, lf_lo))

        q_in, k_st, decay = [], [], []
        for j in chunks:
            b_last = b[j][t - 1:t, :]
            q_in.append((q[j] * jnp.exp(b[j])).astype(BF16))
            k_st.append((k[j] * jnp.exp(b_last - b[j])).astype(BF16))
            decay.append(jnp.exp(b_last))

        o_off = [None] * n_group
        if nsb > 1:
            qt, kt, vst = [], [], []
            for j in chunks:
                qt_parts = [jnp.zeros((sub, w), F32)]
                kt_parts = []
                vs_parts = []
                for i in range(1, nsb):
                    ref_row = b[j][i * sub - 1:i * sub, :]
                    qt_parts.append(q[j][i * sub:(i + 1) * sub, :]
                                    * jnp.exp(b[j][i * sub:(i + 1) * sub, :] - ref_row))
                    kt_parts.append(k[j][:i * sub, :] * jnp.exp(ref_row - b[j][:i * sub, :]))
                    vs_parts.append(v[j][:i * sub, :])
                qt.append(jnp.concatenate(qt_parts, axis=0).astype(BF16))
                kt.append(jnp.concatenate(kt_parts, axis=0).astype(BF16))
                vst.append(jnp.concatenate(vs_parts, axis=0).astype(BF16))
            sc = [[_dot_nt(qt[j][:, lanes], kt[j][:, lanes]) for lanes in heads] for j in chunks]
            sc = [[jnp.where(pair, x, 0.0).astype(BF16) for x in row] for row in sc]
            o_off = [jnp.concatenate([_dot(sc[j][h], vst[j][:, lanes]) for h, lanes in enumerate(heads)], axis=1)
                     for j in chunks]

        upd = [[_dot(jnp.transpose(v[j][:, lanes]).astype(BF16), k_st[j][:, lanes]) for lanes in heads]
               for j in chunks]
        states = [[st_ref[0, h] for h in range(HG_HEADS)]]
        for j in chunks:
            states.append([states[j][h] * decay[j][:, lanes] + upd[j][h] for h, lanes in enumerate(heads)])
        for h in range(HG_HEADS):
            st_ref[0, h] = states[n_group][h]
        o_state = [jnp.concatenate([_dot_nt(q_in[j][:, lanes], states[j][h].astype(BF16))
                                    for h, lanes in enumerate(heads)], axis=1) for j in chunks]

        p_all = []
        for j in chunks:
            b3 = b[j].reshape(nsb, sub, w)
            q3 = q[j].reshape(nsb, sub, w)
            k3 = k[j].reshape(nsb, sub, w)
            terms = []
            for s in range(sub):
                bs = jnp.broadcast_to(b3[:, s:s + 1, :], (nsb, sub, w))
                ks = jnp.broadcast_to(k3[:, s:s + 1, :], (nsb, sub, w))
                e = jnp.where(sub_row >= s, jnp.exp(b3 - bs), 0.0)
                terms.append((q3 * ks * e).reshape(nsb * sub, w))
            a_all = jnp.concatenate(terms, axis=0).astype(BF16)
            p_all.append(_head_sums(a_all, ones_wide, ones_narrow))

        for j in chunks:
            v3 = v[j].reshape(nsb, sub, w)
            o = jnp.zeros((nsb, sub, w), F32)
            for s in range(sub):
                vs = jnp.broadcast_to(v3[:, s:s + 1, :], (nsb, sub, w))
                o = o + p_all[j][s * t:(s + 1) * t, :].reshape(nsb, sub, w) * vs
            o = o.reshape(t, w) + o_state[j]
            if nsb > 1:
                o = o + o_off[j]
            sq_hi, sq_lo = _split_hi_lo(o * o)
            ms = (_head_sums(sq_hi, ones_wide, ones_narrow)
                  + _head_sums(sq_lo, ones_wide, ones_narrow)) * (1.0 / HEAD_DIM)
            o_ref[0, rows[j], :] = (o * lax.rsqrt(ms + RMS_EPS) * nw_ref[...]
                                    * gt_ref[0, rows[j], :]).astype(BF16)
        return carry

    lax.fori_loop(0, rb // (t * n_group), group, 0)


def _hgrn_call(qs, kk, hv, lf, gt, norm_w, s0_t, *, t):
    b, l, w = qs.shape
    rb = min(512, l)
    n_group = min(HG_GROUP, rb // t)
    assert l % rb == 0 and rb % (t * n_group) == 0 and t % HG_SUB == 0
    row_spec = pl.BlockSpec((1, rb, w), lambda i, j: (i, j, 0))
    st_spec = pl.BlockSpec((1, HG_HEADS, HEAD_DIM, HEAD_DIM), lambda i, j: (i, 0, 0, 0))
    return pl.pallas_call(
        functools.partial(_hgrn_kernel, rb=rb, t=t, n_group=n_group),
        grid=(b, l // rb),
        in_specs=[row_spec] * 5 + [pl.BlockSpec((1, w), lambda i, j: (0, 0)), st_spec],
        out_specs=[row_spec, st_spec],
        out_shape=[jax.ShapeDtypeStruct((b, l, w), BF16),
                   jax.ShapeDtypeStruct((b, HG_HEADS, HEAD_DIM, HEAD_DIM), F32)],
        compiler_params=pltpu.CompilerParams(dimension_semantics=("parallel", "arbitrary"),
                                             vmem_limit_bytes=VMEM_LIMIT_BYTES),
        name="hgrn",
    )(qs, kk, hv, lf, gt, norm_w, s0_t)


def _ffn_kernel(x_ref, oa_ref, ob_ref, oc_ref, wo_ref, g1_ref, b1_ref, w1_ref, w2_ref, g2_ref, b2_ref,
                y_ref, *, alpha, ff_chunk):
    mix = (_dot(oa_ref[...], wo_ref[0:SB_WIDTH, :])
           + _dot(ob_ref[...], wo_ref[SB_WIDTH:SB_WIDTH + MLP_WIDTH, :])
           + _dot(oc_ref[...], wo_ref[SB_WIDTH + MLP_WIDTH:, :]))
    x1 = _layer_norm(alpha * x_ref[...] + mix, g1_ref[...], b1_ref[...])
    x1b = x1.astype(BF16)
    d_ff = w1_ref.shape[1]
    ff = jnp.zeros_like(x1)
    for c in range(d_ff // ff_chunk):
        cols = slice(c * ff_chunk, (c + 1) * ff_chunk)
        hdn = jnp.maximum(_dot(x1b, w1_ref[:, cols]), 0.0)
        ff = ff + _dot((hdn * hdn).astype(BF16), w2_ref[cols, :])
    y_ref[...] = _layer_norm(alpha * x1 + ff, g2_ref[...], b2_ref[...])


def _ffn_call(x2d, oa, ob, oc, wo, g1, b1, w1, w2, g2, b2, *, alpha):
    n, d = x2d.shape
    d_ff = w1.shape[1]
    tm = min(512, n)
    ff_chunk = min(1024, d_ff)
    assert n % tm == 0 and d_ff % ff_chunk == 0
    row_spec = lambda width: pl.BlockSpec((tm, width), lambda i: (i, 0))
    const = lambda shape: pl.BlockSpec(shape, lambda i: (0,) * len(shape), pipeline_mode=pl.Buffered(1))
    return pl.pallas_call(
        functools.partial(_ffn_kernel, alpha=alpha, ff_chunk=ff_chunk),
        grid=(n // tm,),
        in_specs=[row_spec(d), row_spec(SB_WIDTH), row_spec(MLP_WIDTH), row_spec(HG_WIDTH),
                  const(wo.shape), const((1, d)), const((1, d)),
                  const(w1.shape), const(w2.shape), const((1, d)), const((1, d))],
        out_specs=row_spec(d),
        out_shape=jax.ShapeDtypeStruct((n, d), F32),
        compiler_params=pltpu.CompilerParams(dimension_semantics=("parallel",),
                                             vmem_limit_bytes=VMEM_LIMIT_BYTES),
        name="ffn",
    )(x2d, oa, ob, oc, wo, g1, b1, w1, w2, g2, b2)


def _trunk(x, past_k, past_v, hg_state, w_in, w_out, mlp_ln_g, mlp_ln_b, mlp_ws, mlp_bs,
           hg_lb_logits, hg_norm_w, ln1_g, ln1_b, w_ff1, w_ff2, ln2_g, ln2_b):
    bsz, l, d = x.shape
    depth = w_in.shape[0]
    n = bsz * l
    alpha = (2 * depth) ** 0.25
    prompt = past_k is None
    cl = MLP_CHUNK if prompt else l
    assert MLP_CHUNK % cl == 0 and l % cl == 0
    reps = MLP_CHUNK // cl

    g = jax.nn.softmax(hg_lb_logits.astype(F32), axis=0)
    cs = jnp.cumsum(g, axis=0)
    lbs = cs - cs[0:1]

    x2d = x.reshape(n, d)
    carried_kv = ()
    new_s, new_mv = [], []
    for layer in range(depth):
        lb = lbs[layer][None, :]
        wm_tiled = jnp.tile(mlp_ws[layer][:, :cl, :cl], (1, reps, reps))
        brow = jnp.tile(jnp.repeat(jnp.transpose(mlp_bs[layer][:, :cl]), HEAD_DIM, axis=1), (reps, 1))
        outs = _proj_call(x2d, w_in[layer].astype(BF16), mlp_ln_g[layer][None], mlp_ln_b[layer][None],
                          wm_tiled, brow, jnp.log(lb), jnp.log1p(-lb), 1.0 - lb, carried_kv,
                          layer=layer, depth=depth, cl=cl, emit_vn=not prompt)
        carried_kv = tuple(outs[:2])
        qa, kab, vab, ob, qs, kk, hv, lf, gt = outs[2:11]
        shp = lambda a: a.reshape(bsz, l, a.shape[-1])
        if prompt:
            oa = _attn_call(shp(qa), shp(kab), shp(vab), q_start=0)
            s0_t = jnp.zeros((bsz, HG_HEADS, HEAD_DIM, HEAD_DIM), F32)
            t = HG_CHUNK
        else:
            past_len = past_k.shape[2]
            pad = (-(past_len + l)) % KEY_BLOCK
            cat = lambda past, new: jnp.pad(
                jnp.concatenate([past.reshape(bsz, past_len, SB_WIDTH).astype(BF16), shp(new)], axis=1),
                ((0, 0), (0, pad), (0, 0)))
            oa = _attn_call(shp(qa), cat(past_k[layer], kab), cat(past_v[layer], vab), q_start=past_len)
            s0_t = jnp.swapaxes(hg_state[layer].astype(F32), -1, -2)
            t = l
            new_mv.append(outs[11].reshape(bsz, l, MLP_WIDTH))
        oc, s_t = _hgrn_call(shp(qs), shp(kk), shp(hv), shp(lf), shp(gt), hg_norm_w[layer][None], s0_t, t=t)
        x2d = _ffn_call(x2d, oa.reshape(n, SB_WIDTH), ob, oc.reshape(n, HG_WIDTH),
                        w_out[layer].astype(BF16), ln1_g[layer][None], ln1_b[layer][None],
                        w_ff1[layer].astype(BF16), w_ff2[layer].astype(BF16),
                        ln2_g[layer][None], ln2_b[layer][None], alpha=alpha)
        new_s.append(jnp.swapaxes(s_t, -1, -2))
    mv = None if prompt else jnp.stack(new_mv)
    new_k, new_v = (a.reshape(depth, bsz, l, SB_HEADS, HEAD_DIM) for a in carried_kv)
    return x2d.reshape(bsz, l, d), new_k, new_v, jnp.stack(new_s), mv


def kernel(x_prompt, x_sample, cache_sb_k, cache_sb_v, state_hgrn, w_in, w_out, mlp_ln_g, mlp_ln_b, mlp_ws, mlp_bs, hg_lb_logits, hg_norm_w, ln1_g, ln1_b, w_ff1, w_ff2, ln2_g, ln2_b):
    weights = (w_in, w_out, mlp_ln_g, mlp_ln_b, mlp_ws, mlp_bs, hg_lb_logits, hg_norm_w,
               ln1_g, ln1_b, w_ff1, w_ff2, ln2_g, ln2_b)
    y_p, k_p, v_p, s_p, _ = _trunk(x_prompt, None, None, None, *weights)
    y_s, k_s, v_s, s_s, mv_s = _trunk(x_sample, cache_sb_k, cache_sb_v, state_hgrn, *weights)
    return (y_p, y_s, k_p, v_p, s_p, k_s, v_s, s_s, mv_s)
```

```python
import functools
import math

import jax
import jax.numpy as jnp
from jax import lax
from jax.experimental import pallas as pl
from jax.experimental.pallas import tpu as pltpu

F32 = jnp.float32
BF16 = jnp.bfloat16

HEAD_DIM = 64
SB_HEADS = 6
SB_WIDTH = SB_HEADS * HEAD_DIM
MLP_GROUPS = 4
MLP_WIDTH = MLP_GROUPS * HEAD_DIM
MLP_CHUNK = 128
HG_HEADS = 6
HG_WIDTH = HG_HEADS * HEAD_DIM
HG_CHUNK = 64
HG_SUB = 8
HG_GROUP = 4
MXU_WIDTH = 256
KEY_BLOCK = 128
SUPER_KEYS = 256
UNDERFLOW_LOG2 = -160.0
LOG2E = 1.4426950408889634
LN_EPS = 1e-5
RMS_EPS = 1e-6

OFF_QA = 0
OFF_KA = OFF_QA + SB_WIDTH
OFF_VA = OFF_KA + SB_WIDTH
OFF_UB = OFF_VA + SB_WIDTH
OFF_VB = OFF_UB + MLP_WIDTH
OFF_QC = OFF_VB + MLP_WIDTH
OFF_FC = OFF_QC + HG_WIDTH
OFF_IC = OFF_FC + HG_WIDTH
OFF_GC = OFF_IC + HG_WIDTH
IN_WIDTH = OFF_GC + HG_WIDTH

V7X_VMEM_BYTES = 64 * 1024 * 1024
VMEM_LIMIT_BYTES = V7X_VMEM_BYTES - 8 * 1024 * 1024


def _dot(a, b):
    return jnp.dot(a, b, preferred_element_type=F32)


def _dot_nt(a, b):
    return lax.dot_general(a, b, (((1,), (1,)), ((), ())), preferred_element_type=F32)


def _sigmoid(x):
    return 1.0 / (1.0 + jnp.exp(-x))


def _log_sigmoid(x):
    return jnp.minimum(x, 0.0) - jnp.log1p(jnp.exp(-jnp.abs(x)))


def _layer_norm(x, g, b):
    mu = jnp.mean(x, axis=-1, keepdims=True)
    xc = x - mu
    var = jnp.mean(xc * xc, axis=-1, keepdims=True)
    return xc * lax.rsqrt(var + LN_EPS) * g + b


def _split_hi_lo(x):
    hi = x.astype(BF16)
    lo = (x - hi.astype(F32)).astype(BF16)
    return hi, lo


def _head_block_ones(width):
    r = lax.broadcasted_iota(jnp.int32, (width, width), 0) // HEAD_DIM
    c = lax.broadcasted_iota(jnp.int32, (width, width), 1) // HEAD_DIM
    return jnp.where(r == c, 1.0, 0.0).astype(BF16)


def _head_sums(x, ones_wide, ones_narrow):
    split = ones_wide.shape[0]
    return jnp.concatenate([_dot(x[:, :split], ones_wide), _dot(x[:, split:], ones_narrow)], axis=1)


def _proj_kernel(x_ref, w_ref, lng_ref, lnb_ref, wm_ref, brow_ref, loglb_ref, l1mlb_ref, omlb_ref,
                 *rest, tm, cl, n_carried):
    (ka_ref, va_ref, qa_ref, kab_ref, vab_ref, ob_ref, qs_ref, kk_ref, hv_ref, lf_ref,
     gt_ref, *maybe_vn_ref) = rest[n_carried:]
    xb = x_ref[...].astype(BF16)

    n_tiles = pl.cdiv(IN_WIDTH, MXU_WIDTH)
    first = OFF_QC // MXU_WIDTH
    tiles = {}
    for i in list(range(first, n_tiles)) + list(range(first)):
        tiles[i] = _dot(xb, w_ref[:, i * MXU_WIDTH:min((i + 1) * MXU_WIDTH, IN_WIDTH)])

    def proj(off, width):
        parts = []
        pos = off
        while pos < off + width:
            i, lo = divmod(pos, MXU_WIDTH)
            take = min(MXU_WIDTH - lo, off + width - pos)
            parts.append(tiles[i][:, lo:lo + take])
            pos += take
        return parts[0] if len(parts) == 1 else jnp.concatenate(parts, axis=1)

    qc = proj(OFF_QC, HG_WIDTH)
    qs_ref[...] = qc * _sigmoid(qc)
    zf = proj(OFF_FC, HG_WIDTH)
    c_term = l1mlb_ref[...] + _log_sigmoid(zf)
    a_term = loglb_ref[...]
    lf_ref[...] = jnp.maximum(a_term, c_term) + jnp.log1p(jnp.exp(-jnp.abs(a_term - c_term)))
    kk_ref[...] = omlb_ref[...] * _sigmoid(-zf)
    hv_ref[...] = proj(OFF_IC, HG_WIDTH)
    gt_ref[...] = _sigmoid(proj(OFF_GC, HG_WIDTH))

    vn = _layer_norm(proj(OFF_VB, MLP_WIDTH), lng_ref[...], lnb_ref[...])
    if maybe_vn_ref:
        maybe_vn_ref[0][...] = vn
    ub = proj(OFF_UB, MLP_WIDTH)
    row = lax.broadcasted_iota(jnp.int32, (MLP_CHUNK, MLP_CHUNK), 0)
    col = lax.broadcasted_iota(jnp.int32, (MLP_CHUNK, MLP_CHUNK), 1)
    keep = (col <= row) & ((row // cl) == (col // cl))
    wm = [jnp.where(keep, wm_ref[g], 0.0).astype(BF16) for g in range(MLP_GROUPS)]
    left_head = lax.broadcasted_iota(jnp.int32, (MLP_CHUNK, 2 * HEAD_DIM), 1) < HEAD_DIM
    brow = brow_ref[...]
    vnb = vn.astype(BF16)
    for c in range(tm // MLP_CHUNK):
        rows = slice(c * MLP_CHUNK, (c + 1) * MLP_CHUNK)
        for p in range(MLP_GROUPS // 2):
            lanes = slice(p * 2 * HEAD_DIM, (p + 1) * 2 * HEAD_DIM)
            vp = vnb[rows, lanes]
            mixed = jnp.where(left_head, _dot(wm[2 * p], vp), _dot(wm[2 * p + 1], vp)) + brow[:, lanes]
            ob_ref[rows, lanes] = (ub[rows, lanes] * mixed).astype(BF16)

    qa_ref[...] = (proj(OFF_QA, SB_WIDTH) * (LOG2E / math.sqrt(HEAD_DIM))).astype(BF16)
    ka = proj(OFF_KA, SB_WIDTH)
    ka_ref[...] = ka
    kab_ref[...] = ka.astype(BF16)
    va = proj(OFF_VA, SB_WIDTH)
    va_ref[...] = va
    vab_ref[...] = va.astype(BF16)


def _proj_call(x2d, w_in_bf, ln_g, ln_b, wm_tiled, brow, loglb, l1mlb, omlb, carried_kv, *,
               layer, depth, cl, emit_vn):
    n, d = x2d.shape
    tm = min(512, n)
    assert n % tm == 0 and tm % MLP_CHUNK == 0 and MLP_CHUNK % cl == 0
    row_spec = lambda width: pl.BlockSpec((tm, width), lambda i: (i, 0))
    layer_spec = pl.BlockSpec((None, tm, SB_WIDTH), lambda i: (layer, i, 0))
    const = lambda shape: pl.BlockSpec(shape, lambda i: (0,) * len(shape), pipeline_mode=pl.Buffered(1))
    out_shapes = [
        jax.ShapeDtypeStruct((depth, n, SB_WIDTH), F32),
        jax.ShapeDtypeStruct((depth, n, SB_WIDTH), F32),
        jax.ShapeDtypeStruct((n, SB_WIDTH), BF16),
        jax.ShapeDtypeStruct((n, SB_WIDTH), BF16),
        jax.ShapeDtypeStruct((n, SB_WIDTH), BF16),
        jax.ShapeDtypeStruct((n, MLP_WIDTH), BF16),
        jax.ShapeDtypeStruct((n, HG_WIDTH), F32),
        jax.ShapeDtypeStruct((n, HG_WIDTH), F32),
        jax.ShapeDtypeStruct((n, HG_WIDTH), F32),
        jax.ShapeDtypeStruct((n, HG_WIDTH), F32),
        jax.ShapeDtypeStruct((n, HG_WIDTH), F32),
    ]
    if emit_vn:
        out_shapes.append(jax.ShapeDtypeStruct((n, MLP_WIDTH), F32))
    out_specs = [layer_spec, layer_spec] + [row_spec(s.shape[1]) for s in out_shapes[2:]]
    n_fixed = 9
    return pl.pallas_call(
        functools.partial(_proj_kernel, tm=tm, cl=cl, n_carried=len(carried_kv)),
        grid=(n // tm,),
        in_specs=[
            row_spec(d),
            const((d, IN_WIDTH)),
            const((1, MLP_WIDTH)), const((1, MLP_WIDTH)),
            const((MLP_GROUPS, MLP_CHUNK, MLP_CHUNK)),
            const((MLP_CHUNK, MLP_WIDTH)),
            const((1, HG_WIDTH)), const((1, HG_WIDTH)), const((1, HG_WIDTH)),
        ] + [pl.BlockSpec(memory_space=pl.ANY)] * len(carried_kv),
        out_specs=out_specs,
        out_shape=out_shapes,
        input_output_aliases={n_fixed + j: j for j in range(len(carried_kv))},
        compiler_params=pltpu.CompilerParams(dimension_semantics=("parallel",),
                                             vmem_limit_bytes=VMEM_LIMIT_BYTES),
        name="proj",
    )(x2d, w_in_bf, ln_g, ln_b, wm_tiled, brow, loglb, l1mlb, omlb, *carried_kv)


def _sb_scores(z, mask, uu):
    kb = KEY_BLOCK
    log_beta = jnp.minimum(z, 0.0) - jnp.log2(1.0 + jnp.exp2(-jnp.abs(z)))
    log_keep = log_beta - z
    if mask is not None:
        log_keep = jnp.where(mask, log_keep, 0.0)
    hi, lo = _split_hi_lo(log_keep)
    sums = _dot(jnp.concatenate([hi, lo], axis=1), uu)
    return log_beta + sums[:, :kb], sums[:, kb:]


def _sb_finish(pre, total, mask, run):
    w = jnp.exp2(pre + run)
    if mask is not None:
        w = jnp.where(mask, w, 0.0)
    return w.astype(BF16), run + total


def _attn_kernel(q_ref, k_ref, v_ref, o_ref, run_ref, acc_ref, *, tq, nstrip, nsuper, q_start):
    kb = KEY_BLOCK
    r = lax.broadcasted_iota(jnp.int32, (2 * kb, 2 * kb), 0) % kb
    c = lax.broadcasted_iota(jnp.int32, (2 * kb, 2 * kb), 1)
    uu = jnp.where((c >= kb) | (r > c), 1.0, 0.0).astype(BF16)
    n_rows = nstrip * tq
    row = lax.broadcasted_iota(jnp.int32, (n_rows, kb), 0)
    col = lax.broadcasted_iota(jnp.int32, (n_rows, kb), 1)
    own_mask = (row >= tq) | (col < row)
    heads = [slice(h * HEAD_DIM, (h + 1) * HEAD_DIM) for h in range(2)]
    n_before = q_start // SUPER_KEYS

    nblk = SUPER_KEYS // kb

    def aligned(x, m):
        return x if isinstance(x, int) else pl.multiple_of(x, m)

    def q_super(qi, has_earlier):
        q0 = aligned(qi * n_rows, n_rows)
        n_earlier = n_before + qi
        kd0 = aligned(n_earlier * SUPER_KEYS, SUPER_KEYS)
        run_ref[...] = jnp.zeros_like(run_ref)
        acc_ref[...] = jnp.zeros_like(acc_ref)

        def own_scores(g):
            lo = g * tq
            out = []
            for lanes in heads:
                q = q_ref[0, pl.ds(q0 + lo, n_rows - lo), lanes]
                k = k_ref[0, pl.ds(kd0 + g * kb, kb), lanes]
                out.append(_sb_scores(_dot_nt(q, k), own_mask[:n_rows - lo], uu))
            return out

        def own_finish(g, scores):
            lo = g * tq
            for h, lanes in enumerate(heads):
                pre, total = scores[h]
                w, run = _sb_finish(pre, total, own_mask[:n_rows - lo], run_ref[h, lo:, :])
                run_ref[h, lo:, :] = run
                acc_ref[h, lo:, :] += _dot(w, v_ref[0, pl.ds(kd0 + g * kb, kb), lanes])

        def earlier_scores(k0):
            out = []
            for lanes in heads:
                z = _dot_nt(q_ref[0, pl.ds(q0, n_rows), lanes], k_ref[0, pl.ds(k0, SUPER_KEYS), lanes])
                out.append([_sb_scores(z[:, g * kb:(g + 1) * kb], None, uu) for g in range(nblk)])
            return out

        def earlier_finish(k0, scores):
            for h, lanes in enumerate(heads):
                run = run_ref[h]
                ws = [None] * nblk
                for g in reversed(range(nblk)):
                    ws[g], run = _sb_finish(*scores[h][g], None, run)
                run_ref[h] = run
                acc_ref[h] += _dot(jnp.concatenate(ws, axis=1), v_ref[0, pl.ds(k0, SUPER_KEYS), lanes])

        k_prev = pl.multiple_of((n_earlier - 1) * SUPER_KEYS, SUPER_KEYS) if has_earlier else None
        own = [own_scores(g) for g in range(nstrip)]
        prev = earlier_scores(k_prev) if has_earlier else None
        for g in reversed(range(nstrip)):
            own_finish(g, own[g])
        if has_earlier:
            earlier_finish(k_prev, prev)

            def alive(c):
                return (c[0] < n_earlier) & (c[1] > UNDERFLOW_LOG2)

            def k_super(c):
                k0 = pl.multiple_of((n_earlier - 1 - c[0]) * SUPER_KEYS, SUPER_KEYS)
                earlier_finish(k0, earlier_scores(k0))
                return c[0] + 1, jnp.max(run_ref[...])

            lax.while_loop(alive, k_super, (jnp.int32(1), jnp.max(run_ref[...])))
        o_ref[0, pl.ds(q0, n_rows), :] = jnp.concatenate([acc_ref[0], acc_ref[1]], axis=1).astype(BF16)

    def q_super_step(qi, carry):
        q_super(qi, True)
        return carry

    if n_before == 0:
        q_super(0, False)
        lax.fori_loop(1, nsuper, q_super_step, 0)
    else:
        lax.fori_loop(0, nsuper, q_super_step, 0)


def _attn_call(q, k, v, *, q_start):
    b, lq, _ = q.shape
    lk = k.shape[1]
    if lq % SUPER_KEYS == 0:
        tq, nstrip, nsuper = KEY_BLOCK, SUPER_KEYS // KEY_BLOCK, lq // SUPER_KEYS
    else:
        tq, nstrip, nsuper = lq, 1, 1
    assert tq <= KEY_BLOCK and q_start % SUPER_KEYS == 0 and q_start + lq <= lk and lk % KEY_BLOCK == 0
    spec = lambda l: pl.BlockSpec((1, l, 2 * HEAD_DIM), lambda i, p: (i, 0, p))
    return pl.pallas_call(
        functools.partial(_attn_kernel, tq=tq, nstrip=nstrip, nsuper=nsuper, q_start=q_start),
        grid=(b, SB_HEADS // 2),
        in_specs=[spec(lq), spec(lk), spec(lk)],
        out_specs=spec(lq),
        out_shape=jax.ShapeDtypeStruct((b, lq, SB_WIDTH), BF16),
        scratch_shapes=[pltpu.VMEM((2, nstrip * tq, KEY_BLOCK), F32),
                        pltpu.VMEM((2, nstrip * tq, HEAD_DIM), F32)],
        compiler_params=pltpu.CompilerParams(dimension_semantics=("parallel", "parallel"),
                                             vmem_limit_bytes=VMEM_LIMIT_BYTES),
        name="attn",
    )(q, k, v)


def _hgrn_kernel(qs_ref, kk_ref, hv_ref, lf_ref, gt_ref, nw_ref, s0_ref, o_ref, st_ref, *, rb, t, n_group):
    nsb = t // HG_SUB
    sub = HG_SUB
    w = HG_WIDTH
    heads = [slice(h * HEAD_DIM, (h + 1) * HEAD_DIM) for h in range(HG_HEADS)]

    @pl.when(pl.program_id(1) == 0)
    def _():
        st_ref[...] = s0_ref[...]

    ones_wide = _head_block_ones(MXU_WIDTH)
    ones_narrow = _head_block_ones(w - MXU_WIDTH)
    tri_incl = jnp.where(lax.broadcasted_iota(jnp.int32, (t, t), 1) <= lax.broadcasted_iota(jnp.int32, (t, t), 0),
                         1.0, 0.0).astype(BF16)
    sub_row = lax.broadcasted_iota(jnp.int32, (nsb, sub, w), 1)
    if nsb > 1:
        n_stack = sub * (nsb * (nsb - 1) // 2)
        seg_start = [sub * (i * (i - 1) // 2) for i in range(nsb + 1)]
        tr = lax.broadcasted_iota(jnp.int32, (t, n_stack), 0) // sub
        tc = lax.broadcasted_iota(jnp.int32, (t, n_stack), 1)
        pair = jnp.zeros((t, n_stack), jnp.bool_)
        for i in range(1, nsb):
            pair = pair | ((tr == i) & (tc >= seg_start[i]) & (tc < seg_start[i + 1]))

    def group(gi, carry):
        chunks = range(n_group)
        rows = [pl.ds(pl.multiple_of((gi * n_group + j) * t, t), t) for j in chunks]
        q = [qs_ref[0, r, :] for r in rows]
        k = [kk_ref[0, r, :] for r in rows]
        v = [hv_ref[0, r, :] for r in rows]

        b = []
        for r in rows:
            lf_hi, lf_lo = _split_hi_lo(lf_ref[0, r, :])
            b.append(_dot(tri_incl, lf_hi) + _dot(tri_incl, lf_lo))

        q_in, k_st, decay = [], [], []
        for j in chunks:
            b_last = b[j][t - 1:t, :]
            q_in.append((q[j] * jnp.exp(b[j])).astype(BF16))
            k_st.append((k[j] * jnp.exp(b_last - b[j])).astype(BF16))
            decay.append(jnp.exp(b_last))

        o_off = [None] * n_group
        if nsb > 1:
            qt, kt, vst = [], [], []
            for j in chunks:
                qt_parts = [jnp.zeros((sub, w), F32)]
                kt_parts = []
                vs_parts = []
                for i in range(1, nsb):
                    ref_row = b[j][i * sub - 1:i * sub, :]
                    qt_parts.append(q[j][i * sub:(i + 1) * sub, :]
                                    * jnp.exp(b[j][i * sub:(i + 1) * sub, :] - ref_row))
                    kt_parts.append(k[j][:i * sub, :] * jnp.exp(ref_row - b[j][:i * sub, :]))
                    vs_parts.append(v[j][:i * sub, :])
                qt.append(jnp.concatenate(qt_parts, axis=0).astype(BF16))
                kt.append(jnp.concatenate(kt_parts, axis=0).astype(BF16))
                vst.append(jnp.concatenate(vs_parts, axis=0).astype(BF16))
            sc = [[_dot_nt(qt[j][:, lanes], kt[j][:, lanes]) for lanes in heads] for j in chunks]
            sc = [[jnp.where(pair, x, 0.0).astype(BF16) for x in row] for row in sc]
            o_off = [jnp.concatenate([_dot(sc[j][h], vst[j][:, lanes]) for h, lanes in enumerate(heads)], axis=1)
                     for j in chunks]

        upd = [[_dot(jnp.transpose(v[j][:, lanes]).astype(BF16), k_st[j][:, lanes]) for lanes in heads]
               for j in chunks]
        states = [[st_ref[0, h] for h in range(HG_HEADS)]]
        for j in chunks:
            states.append([states[j][h] * decay[j][:, lanes] + upd[j][h] for h, lanes in enumerate(heads)])
        for h in range(HG_HEADS):
            st_ref[0, h] = states[n_group][h]
        o_state = [jnp.concatenate([_dot_nt(q_in[j][:, lanes], states[j][h].astype(BF16))
                                    for h, lanes in enumerate(heads)], axis=1) for j in chunks]

        p_all = []
        for j in chunks:
            b3 = b[j].reshape(nsb, sub, w)
            q3 = q[j].reshape(nsb, sub, w)
            k3 = k[j].reshape(nsb, sub, w)
            terms = []
            for s in range(sub):
                bs = jnp.broadcast_to(b3[:, s:s + 1, :], (nsb, sub, w))
                ks = jnp.broadcast_to(k3[:, s:s + 1, :], (nsb, sub, w))
                e = jnp.where(sub_row >= s, jnp.exp(b3 - bs), 0.0)
                terms.append((q3 * ks * e).reshape(nsb * sub, w))
            a_all = jnp.concatenate(terms, axis=0).astype(BF16)
            p_all.append(_head_sums(a_all, ones_wide, ones_narrow))

        for j in chunks:
            v3 = v[j].reshape(nsb, sub, w)
            o = jnp.zeros((nsb, sub, w), F32)
            for s in range(sub):
                vs = jnp.broadcast_to(v3[:, s:s + 1, :], (nsb, sub, w))
                o = o + p_all[j][s * t:(s + 1) * t, :].reshape(nsb, sub, w) * vs
            o = o.reshape(t, w) + o_state[j]
            if nsb > 1:
                o = o + o_off[j]
            sq_hi, sq_lo = _split_hi_lo(o * o)
            ms = (_head_sums(sq_hi, ones_wide, ones_narrow)
                  + _head_sums(sq_lo, ones_wide, ones_narrow)) * (1.0 / HEAD_DIM)
            o_ref[0, rows[j], :] = (o * lax.rsqrt(ms + RMS_EPS) * nw_ref[...]
                                    * gt_ref[0, rows[j], :]).astype(BF16)
        return carry

    lax.fori_loop(0, rb // (t * n_group), group, 0)


def _hgrn_call(qs, kk, hv, lf, gt, norm_w, s0_t, *, t):
    b, l, w = qs.shape
    rb = min(512, l)
    n_group = min(HG_GROUP, rb // t)
    assert l % rb == 0 and rb % (t * n_group) == 0 and t % HG_SUB == 0
    row_spec = pl.BlockSpec((1, rb, w), lambda i, j: (i, j, 0))
    st_spec = pl.BlockSpec((1, HG_HEADS, HEAD_DIM, HEAD_DIM), lambda i, j: (i, 0, 0, 0))
    return pl.pallas_call(
        functools.partial(_hgrn_kernel, rb=rb, t=t, n_group=n_group),
        grid=(b, l // rb),
        in_specs=[row_spec] * 5 + [pl.BlockSpec((1, w), lambda i, j: (0, 0)), st_spec],
        out_specs=[row_spec, st_spec],
        out_shape=[jax.ShapeDtypeStruct((b, l, w), BF16),
                   jax.ShapeDtypeStruct((b, HG_HEADS, HEAD_DIM, HEAD_DIM), F32)],
        compiler_params=pltpu.CompilerParams(dimension_semantics=("parallel", "arbitrary"),
                                             vmem_limit_bytes=VMEM_LIMIT_BYTES),
        name="hgrn",
    )(qs, kk, hv, lf, gt, norm_w, s0_t)


def _ffn_kernel(x_ref, oa_ref, ob_ref, oc_ref, wo_ref, g1_ref, b1_ref, w1_ref, w2_ref, g2_ref, b2_ref,
                y_ref, *, alpha, ff_chunk):
    mix = (_dot(oa_ref[...], wo_ref[0:SB_WIDTH, :])
           + _dot(ob_ref[...], wo_ref[SB_WIDTH:SB_WIDTH + MLP_WIDTH, :])
           + _dot(oc_ref[...], wo_ref[SB_WIDTH + MLP_WIDTH:, :]))
    x1 = _layer_norm(alpha * x_ref[...] + mix, g1_ref[...], b1_ref[...])
    x1b = x1.astype(BF16)
    d_ff = w1_ref.shape[1]
    ff = jnp.zeros_like(x1)
    for c in range(d_ff // ff_chunk):
        cols = slice(c * ff_chunk, (c + 1) * ff_chunk)
        hdn = jnp.maximum(_dot(x1b, w1_ref[:, cols]), 0.0)
        ff = ff + _dot((hdn * hdn).astype(BF16), w2_ref[cols, :])
    y_ref[...] = _layer_norm(alpha * x1 + ff, g2_ref[...], b2_ref[...])


def _ffn_call(x2d, oa, ob, oc, wo, g1, b1, w1, w2, g2, b2, *, alpha):
    n, d = x2d.shape
    d_ff = w1.shape[1]
    tm = min(512, n)
    ff_chunk = min(1024, d_ff)
    assert n % tm == 0 and d_ff % ff_chunk == 0
    row_spec = lambda width: pl.BlockSpec((tm, width), lambda i: (i, 0))
    const = lambda shape: pl.BlockSpec(shape, lambda i: (0,) * len(shape), pipeline_mode=pl.Buffered(1))
    return pl.pallas_call(
        functools.partial(_ffn_kernel, alpha=alpha, ff_chunk=ff_chunk),
        grid=(n // tm,),
        in_specs=[row_spec(d), row_spec(SB_WIDTH), row_spec(MLP_WIDTH), row_spec(HG_WIDTH),
                  const(wo.shape), const((1, d)), const((1, d)),
                  const(w1.shape), const(w2.shape), const((1, d)), const((1, d))],
        out_specs=row_spec(d),
        out_shape=jax.ShapeDtypeStruct((n, d), F32),
        compiler_params=pltpu.CompilerParams(dimension_semantics=("parallel",),
                                             vmem_limit_bytes=VMEM_LIMIT_BYTES),
        name="ffn",
    )(x2d, oa, ob, oc, wo, g1, b1, w1, w2, g2, b2)


def _trunk(x, past_k, past_v, hg_state, w_in, w_out, mlp_ln_g, mlp_ln_b, mlp_ws, mlp_bs,
           hg_lb_logits, hg_norm_w, ln1_g, ln1_b, w_ff1, w_ff2, ln2_g, ln2_b):
    bsz, l, d = x.shape
    depth = w_in.shape[0]
    n = bsz * l
    alpha = (2 * depth) ** 0.25
    prompt = past_k is None
    cl = MLP_CHUNK if prompt else l
    assert MLP_CHUNK % cl == 0 and l % cl == 0
    reps = MLP_CHUNK // cl

    g = jax.nn.softmax(hg_lb_logits.astype(F32), axis=0)
    cs = jnp.cumsum(g, axis=0)
    lbs = cs - cs[0:1]

    x2d = x.reshape(n, d)
    carried_kv = ()
    new_s, new_mv = [], []
    for layer in range(depth):
        lb = lbs[layer][None, :]
        wm_tiled = jnp.tile(mlp_ws[layer][:, :cl, :cl], (1, reps, reps))
        brow = jnp.tile(jnp.repeat(jnp.transpose(mlp_bs[layer][:, :cl]), HEAD_DIM, axis=1), (reps, 1))
        outs = _proj_call(x2d, w_in[layer].astype(BF16), mlp_ln_g[layer][None], mlp_ln_b[layer][None],
                          wm_tiled, brow, jnp.log(lb), jnp.log1p(-lb), 1.0 - lb, carried_kv,
                          layer=layer, depth=depth, cl=cl, emit_vn=not prompt)
        carried_kv = tuple(outs[:2])
        qa, kab, vab, ob, qs, kk, hv, lf, gt = outs[2:11]
        shp = lambda a: a.reshape(bsz, l, a.shape[-1])
        if prompt:
            oa = _attn_call(shp(qa), shp(kab), shp(vab), q_start=0)
            s0_t = jnp.zeros((bsz, HG_HEADS, HEAD_DIM, HEAD_DIM), F32)
            t = HG_CHUNK
        else:
            past_len = past_k.shape[2]
            pad = (-(past_len + l)) % KEY_BLOCK
            cat = lambda past, new: jnp.pad(
                jnp.concatenate([past.reshape(bsz, past_len, SB_WIDTH).astype(BF16), shp(new)], axis=1),
                ((0, 0), (0, pad), (0, 0)))
            oa = _attn_call(shp(qa), cat(past_k[layer], kab), cat(past_v[layer], vab), q_start=past_len)
            s0_t = jnp.swapaxes(hg_state[layer].astype(F32), -1, -2)
            t = l
            new_mv.append(outs[11].reshape(bsz, l, MLP_WIDTH))
        oc, s_t = _hgrn_call(shp(qs), shp(kk), shp(hv), shp(lf), shp(gt), hg_norm_w[layer][None], s0_t, t=t)
        x2d = _ffn_call(x2d, oa.reshape(n, SB_WIDTH), ob, oc.reshape(n, HG_WIDTH),
                        w_out[layer].astype(BF16), ln1_g[layer][None], ln1_b[layer][None],
                        w_ff1[layer].astype(BF16), w_ff2[layer].astype(BF16),
                        ln2_g[layer][None], ln2_b[layer][None], alpha=alpha)
        new_s.append(jnp.swapaxes(s_t, -1, -2))
    mv = None if prompt else jnp.stack(new_mv)
    new_k, new_v = (a.reshape(depth, bsz, l, SB_HEADS, HEAD_DIM) for a in carried_kv)
    return x2d.reshape(bsz, l, d), new_k, new_v, jnp.stack(new_s), mv


def kernel(x_prompt, x_sample, cache_sb_k, cache_sb_v, state_hgrn, w_in, w_out, mlp_ln_g, mlp_ln_b, mlp_ws, mlp_bs, hg_lb_logits, hg_norm_w, ln1_g, ln1_b, w_ff1, w_ff2, ln2_g, ln2_b):
    weights = (w_in, w_out, mlp_ln_g, mlp_ln_b, mlp_ws, mlp_bs, hg_lb_logits, hg_norm_w,
               ln1_g, ln1_b, w_ff1, w_ff2, ln2_g, ln2_b)
    y_p, k_p, v_p, s_p, _ = _trunk(x_prompt, None, None, None, *weights)
    y_s, k_s, v_s, s_s, mv_s = _trunk(x_sample, cache_sb_k, cache_sb_v, state_hgrn, *weights)
    return (y_p, y_s, k_p, v_p, s_p, k_s, v_s, s_s, mv_s)
```

```python
import functools
import math

import jax
import jax.numpy as jnp
from jax import lax
from jax.experimental import pallas as pl
from jax.experimental.pallas import tpu as pltpu

F32 = jnp.float32
BF16 = jnp.bfloat16

HEAD_DIM = 64
SB_HEADS = 6
SB_WIDTH = SB_HEADS * HEAD_DIM
MLP_GROUPS = 4
MLP_WIDTH = MLP_GROUPS * HEAD_DIM
MLP_CHUNK = 128
HG_HEADS = 6
HG_WIDTH = HG_HEADS * HEAD_DIM
HG_CHUNK = 64
HG_SUB = 8
HG_GROUP = 4
MXU_WIDTH = 256
KEY_BLOCK = 128
SUPER_KEYS = 256
UNDERFLOW_LOG2 = -160.0
LOG2E = 1.4426950408889634
LN_EPS = 1e-5
RMS_EPS = 1e-6

OFF_QA = 0
OFF_KA = OFF_QA + SB_WIDTH
OFF_VA = OFF_KA + SB_WIDTH
OFF_UB = OFF_VA + SB_WIDTH
OFF_VB = OFF_UB + MLP_WIDTH
OFF_QC = OFF_VB + MLP_WIDTH
OFF_FC = OFF_QC + HG_WIDTH
OFF_IC = OFF_FC + HG_WIDTH
OFF_GC = OFF_IC + HG_WIDTH
IN_WIDTH = OFF_GC + HG_WIDTH

V7X_VMEM_BYTES = 64 * 1024 * 1024
VMEM_LIMIT_BYTES = V7X_VMEM_BYTES - 8 * 1024 * 1024


def _dot(a, b):
    return jnp.dot(a, b, preferred_element_type=F32)


def _dot_nt(a, b):
    return lax.dot_general(a, b, (((1,), (1,)), ((), ())), preferred_element_type=F32)


def _sigmoid(x):
    return 1.0 / (1.0 + jnp.exp(-x))


def _log_sigmoid(x):
    return jnp.minimum(x, 0.0) - jnp.log1p(jnp.exp(-jnp.abs(x)))


def _layer_norm(x, g, b):
    mu = jnp.mean(x, axis=-1, keepdims=True)
    xc = x - mu
    var = jnp.mean(xc * xc, axis=-1, keepdims=True)
    return xc * lax.rsqrt(var + LN_EPS) * g + b


def _split_hi_lo(x):
    hi = x.astype(BF16)
    lo = (x - hi.astype(F32)).astype(BF16)
    return hi, lo


def _head_block_ones(width):
    r = lax.broadcasted_iota(jnp.int32, (width, width), 0) // HEAD_DIM
    c = lax.broadcasted_iota(jnp.int32, (width, width), 1) // HEAD_DIM
    return jnp.where(r == c, 1.0, 0.0).astype(BF16)


def _head_sums(x, ones_wide, ones_narrow):
    split = ones_wide.shape[0]
    return jnp.concatenate([_dot(x[:, :split], ones_wide), _dot(x[:, split:], ones_narrow)], axis=1)


def _proj_kernel(x_ref, w_ref, lng_ref, lnb_ref, wm_ref, brow_ref, loglb_ref, l1mlb_ref, omlb_ref,
                 *rest, tm, cl, n_carried, layer, depth, head_major):
    (ka_ref, va_ref, qa_ref, kab_ref, vab_ref, ob_ref, qs_ref, kk_ref, hv_ref, lf_ref,
     gt_ref, *maybe_vn_ref) = rest[n_carried:]
    xb = x_ref[...].astype(BF16)

    n_tiles = pl.cdiv(IN_WIDTH, MXU_WIDTH)
    first = OFF_QC // MXU_WIDTH
    tiles = {}
    for i in list(range(first, n_tiles)) + list(range(first)):
        tiles[i] = _dot(xb, w_ref[:, i * MXU_WIDTH:min((i + 1) * MXU_WIDTH, IN_WIDTH)])

    def proj(off, width):
        parts = []
        pos = off
        while pos < off + width:
            i, lo = divmod(pos, MXU_WIDTH)
            take = min(MXU_WIDTH - lo, off + width - pos)
            parts.append(tiles[i][:, lo:lo + take])
            pos += take
        return parts[0] if len(parts) == 1 else jnp.concatenate(parts, axis=1)

    qc = proj(OFF_QC, HG_WIDTH)
    qs_ref[...] = qc * _sigmoid(qc)
    zf = proj(OFF_FC, HG_WIDTH)
    c_term = l1mlb_ref[...] + _log_sigmoid(zf)
    a_term = loglb_ref[...]
    lf_ref[...] = jnp.maximum(a_term, c_term) + jnp.log1p(jnp.exp(-jnp.abs(a_term - c_term)))
    kk_ref[...] = omlb_ref[...] * _sigmoid(-zf)
    hv_ref[...] = proj(OFF_IC, HG_WIDTH)
    gt_ref[...] = _sigmoid(proj(OFF_GC, HG_WIDTH))

    vn = _layer_norm(proj(OFF_VB, MLP_WIDTH), lng_ref[...], lnb_ref[...])
    if maybe_vn_ref:
        maybe_vn_ref[0][...] = vn
    ub = proj(OFF_UB, MLP_WIDTH)
    row = lax.broadcasted_iota(jnp.int32, (MLP_CHUNK, MLP_CHUNK), 0)
    col = lax.broadcasted_iota(jnp.int32, (MLP_CHUNK, MLP_CHUNK), 1)
    keep = (col <= row) & ((row // cl) == (col // cl))
    wm = [jnp.where(keep, wm_ref[g], 0.0).astype(BF16) for g in range(MLP_GROUPS)]
    left_head = lax.broadcasted_iota(jnp.int32, (MLP_CHUNK, 2 * HEAD_DIM), 1) < HEAD_DIM
    brow = brow_ref[...]
    vnb = vn.astype(BF16)
    for c in range(tm // MLP_CHUNK):
        rows = slice(c * MLP_CHUNK, (c + 1) * MLP_CHUNK)
        for p in range(MLP_GROUPS // 2):
            lanes = slice(p * 2 * HEAD_DIM, (p + 1) * 2 * HEAD_DIM)
            vp = vnb[rows, lanes]
            mixed = jnp.where(left_head, _dot(wm[2 * p], vp), _dot(wm[2 * p + 1], vp)) + brow[:, lanes]
            ob_ref[rows, lanes] = (ub[rows, lanes] * mixed).astype(BF16)

    qa_ref[...] = (proj(OFF_QA, SB_WIDTH) * (LOG2E / math.sqrt(HEAD_DIM))).astype(BF16)
    ka = proj(OFF_KA, SB_WIDTH)
    kab_ref[...] = ka.astype(BF16)
    va = proj(OFF_VA, SB_WIDTH)
    vab_ref[...] = va.astype(BF16)
    for src, dst in ((ka, ka_ref), (va, va_ref)):
        if n_carried == 0:
            for other in range(depth):
                if other != layer:
                    dst[other] = jnp.zeros(dst.shape[1:], F32)
            dst = dst.at[layer]
        if head_major:
            for p in range(SB_HEADS // 2):
                pair_t = jnp.transpose(src[:, p * 2 * HEAD_DIM:(p + 1) * 2 * HEAD_DIM])
                dst[2 * p] = pair_t[:HEAD_DIM]
                dst[2 * p + 1] = pair_t[HEAD_DIM:]
        else:
            dst[...] = src


def _proj_call(x2d, w_in_bf, ln_g, ln_b, wm_tiled, brow, loglb, l1mlb, omlb, carried_kv, *,
               layer, depth, seq_len, cl, emit_vn):
    n, d = x2d.shape
    tm = min(512, n)
    assert n % tm == 0 and tm % MLP_CHUNK == 0 and MLP_CHUNK % cl == 0
    head_major = seq_len % tm == 0
    first = not carried_kv
    row_spec = lambda width: pl.BlockSpec((tm, width), lambda i: (i, 0))
    layer_dim = depth if first else None
    layer_idx = 0 if first else layer
    if head_major:
        per_seq = seq_len // tm
        kv_shape = (depth, n // seq_len, SB_HEADS, HEAD_DIM, seq_len)
        layer_spec = pl.BlockSpec((layer_dim, None, SB_HEADS, HEAD_DIM, tm),
                                  lambda i: (layer_idx, i // per_seq, 0, 0, i % per_seq))
    else:
        kv_shape = (depth, n, SB_WIDTH)
        layer_spec = pl.BlockSpec((layer_dim, tm, SB_WIDTH), lambda i: (layer_idx, i, 0))
    const = lambda shape: pl.BlockSpec(shape, lambda i: (0,) * len(shape), pipeline_mode=pl.Buffered(1))
    out_shapes = [
        jax.ShapeDtypeStruct(kv_shape, F32),
        jax.ShapeDtypeStruct(kv_shape, F32),
        jax.ShapeDtypeStruct((n, SB_WIDTH), BF16),
        jax.ShapeDtypeStruct((n, SB_WIDTH), BF16),
        jax.ShapeDtypeStruct((n, SB_WIDTH), BF16),
        jax.ShapeDtypeStruct((n, MLP_WIDTH), BF16),
        jax.ShapeDtypeStruct((n, HG_WIDTH), F32),
        jax.ShapeDtypeStruct((n, HG_WIDTH), F32),
        jax.ShapeDtypeStruct((n, HG_WIDTH), F32),
        jax.ShapeDtypeStruct((n, HG_WIDTH), F32),
        jax.ShapeDtypeStruct((n, HG_WIDTH), F32),
    ]
    if emit_vn:
        out_shapes.append(jax.ShapeDtypeStruct((n, MLP_WIDTH), F32))
    out_specs = [layer_spec, layer_spec] + [row_spec(s.shape[1]) for s in out_shapes[2:]]
    n_fixed = 9
    return pl.pallas_call(
        functools.partial(_proj_kernel, tm=tm, cl=cl, n_carried=len(carried_kv), layer=layer, depth=depth,
                          head_major=head_major),
        grid=(n // tm,),
        in_specs=[
            row_spec(d),
            const((d, IN_WIDTH)),
            const((1, MLP_WIDTH)), const((1, MLP_WIDTH)),
            const((MLP_GROUPS, MLP_CHUNK, MLP_CHUNK)),
            const((MLP_CHUNK, MLP_WIDTH)),
            const((1, HG_WIDTH)), const((1, HG_WIDTH)), const((1, HG_WIDTH)),
        ] + [pl.BlockSpec(memory_space=pl.ANY)] * len(carried_kv),
        out_specs=out_specs,
        out_shape=out_shapes,
        input_output_aliases={n_fixed + j: j for j in range(len(carried_kv))},
        compiler_params=pltpu.CompilerParams(dimension_semantics=("parallel",),
                                             vmem_limit_bytes=VMEM_LIMIT_BYTES),
        name="proj",
    )(x2d, w_in_bf, ln_g, ln_b, wm_tiled, brow, loglb, l1mlb, omlb, *carried_kv)


def _sb_scores(z, mask, uu):
    kb = KEY_BLOCK
    log_beta = jnp.minimum(z, 0.0) - jnp.log2(1.0 + jnp.exp2(-jnp.abs(z)))
    log_keep = log_beta - z
    if mask is not None:
        log_keep = jnp.where(mask, log_keep, 0.0)
    hi, lo = _split_hi_lo(log_keep)
    sums = _dot(jnp.concatenate([hi, lo], axis=1), uu)
    return log_beta + sums[:, :kb], sums[:, kb:]


def _sb_finish(pre, total, mask, run):
    w = jnp.exp2(pre + run)
    if mask is not None:
        w = jnp.where(mask, w, 0.0)
    return w.astype(BF16), run + total


def _attn_kernel(q_ref, k_ref, v_ref, o_ref, run_ref, acc_ref, *, tq, nstrip, nsuper, q_start):
    kb = KEY_BLOCK
    r = lax.broadcasted_iota(jnp.int32, (2 * kb, 2 * kb), 0) % kb
    c = lax.broadcasted_iota(jnp.int32, (2 * kb, 2 * kb), 1)
    uu = jnp.where((c >= kb) | (r > c), 1.0, 0.0).astype(BF16)
    n_rows = nstrip * tq
    row = lax.broadcasted_iota(jnp.int32, (n_rows, kb), 0)
    col = lax.broadcasted_iota(jnp.int32, (n_rows, kb), 1)
    own_mask = (row >= tq) | (col < row)
    heads = [slice(h * HEAD_DIM, (h + 1) * HEAD_DIM) for h in range(SB_HEADS)]
    n_before = q_start // SUPER_KEYS

    nblk = SUPER_KEYS // kb

    def aligned(x, m):
        return x if isinstance(x, int) else pl.multiple_of(x, m)

    def q_super(qi, has_earlier):
        q0 = aligned(qi * n_rows, n_rows)
        n_earlier = n_before + qi
        kd0 = aligned(n_earlier * SUPER_KEYS, SUPER_KEYS)
        run_ref[...] = jnp.zeros_like(run_ref)
        acc_ref[...] = jnp.zeros_like(acc_ref)

        def own_logits(g):
            lo = g * tq
            return [_dot_nt(q_ref[0, pl.ds(q0 + lo, n_rows - lo), lanes],
                            k_ref[0, pl.ds(kd0 + g * kb, kb), lanes]) for lanes in heads]

        def own_scores(g, logits):
            return [_sb_scores(z, own_mask[:n_rows - g * tq], uu) for z in logits]

        def own_finish(g, scores):
            lo = g * tq
            for h, lanes in enumerate(heads):
                pre, total = scores[h]
                w, run = _sb_finish(pre, total, own_mask[:n_rows - lo], run_ref[h, lo:, :])
                run_ref[h, lo:, :] = run
                acc_ref[h, lo:, :] += _dot(w, v_ref[0, pl.ds(kd0 + g * kb, kb), lanes])

        def earlier_logits(k0):
            return [_dot_nt(q_ref[0, pl.ds(q0, n_rows), lanes], k_ref[0, pl.ds(k0, SUPER_KEYS), lanes])
                    for lanes in heads]

        def earlier_scores(logits):
            return [[_sb_scores(z[:, g * kb:(g + 1) * kb], None, uu) for g in range(nblk)] for z in logits]

        def earlier_finish(k0, scores):
            for h, lanes in enumerate(heads):
                run = run_ref[h]
                ws = [None] * nblk
                for g in reversed(range(nblk)):
                    ws[g], run = _sb_finish(*scores[h][g], None, run)
                run_ref[h] = run
                acc_ref[h] += _dot(jnp.concatenate(ws, axis=1), v_ref[0, pl.ds(k0, SUPER_KEYS), lanes])

        k_prev = pl.multiple_of((n_earlier - 1) * SUPER_KEYS, SUPER_KEYS) if has_earlier else None
        own = [own_logits(g) for g in range(nstrip)]
        prev = earlier_logits(k_prev) if has_earlier else None
        own = [own_scores(g, own[g]) for g in range(nstrip)]
        prev = earlier_scores(prev) if has_earlier else None
        for g in reversed(range(nstrip)):
            own_finish(g, own[g])
        if has_earlier:
            earlier_finish(k_prev, prev)

            def alive(c):
                return (c[0] < n_earlier) & (c[1] > UNDERFLOW_LOG2)

            def k_super(c):
                k0 = pl.multiple_of((n_earlier - 1 - c[0]) * SUPER_KEYS, SUPER_KEYS)
                earlier_finish(k0, earlier_scores(earlier_logits(k0)))
                return c[0] + 1, jnp.max(run_ref[...])

            lax.while_loop(alive, k_super, (jnp.int32(1), jnp.max(run_ref[...])))
        o_ref[0, pl.ds(q0, n_rows), :] = jnp.concatenate(
            [acc_ref[h] for h in range(SB_HEADS)], axis=1).astype(BF16)

    def q_super_step(qi, carry):
        q_super(qi, True)
        return carry

    if n_before == 0:
        q_super(0, False)
        lax.fori_loop(1, nsuper, q_super_step, 0)
    else:
        lax.fori_loop(0, nsuper, q_super_step, 0)


def _attn_call(q, k, v, *, q_start):
    b, lq, _ = q.shape
    lk = k.shape[1]
    if lq % SUPER_KEYS == 0:
        tq, nstrip, nsuper = KEY_BLOCK, SUPER_KEYS // KEY_BLOCK, lq // SUPER_KEYS
    else:
        tq, nstrip, nsuper = lq, 1, 1
    assert tq <= KEY_BLOCK and q_start % SUPER_KEYS == 0 and q_start + lq <= lk and lk % KEY_BLOCK == 0
    spec = lambda l: pl.BlockSpec((1, l, SB_WIDTH), lambda i: (i, 0, 0))
    return pl.pallas_call(
        functools.partial(_attn_kernel, tq=tq, nstrip=nstrip, nsuper=nsuper, q_start=q_start),
        grid=(b,),
        in_specs=[spec(lq), spec(lk), spec(lk)],
        out_specs=spec(lq),
        out_shape=jax.ShapeDtypeStruct((b, lq, SB_WIDTH), BF16),
        scratch_shapes=[pltpu.VMEM((SB_HEADS, nstrip * tq, KEY_BLOCK), F32),
                        pltpu.VMEM((SB_HEADS, nstrip * tq, HEAD_DIM), F32)],
        compiler_params=pltpu.CompilerParams(dimension_semantics=("parallel",),
                                             vmem_limit_bytes=VMEM_LIMIT_BYTES),
        name="attn",
    )(q, k, v)


def _hgrn_kernel(qs_ref, kk_ref, hv_ref, lf_ref, gt_ref, nw_ref, s0_ref, o_ref, st_ref, *, rb, t, n_group):
    nsb = t // HG_SUB
    sub = HG_SUB
    w = HG_WIDTH
    heads = [slice(h * HEAD_DIM, (h + 1) * HEAD_DIM) for h in range(HG_HEADS)]

    @pl.when(pl.program_id(1) == 0)
    def _():
        st_ref[...] = s0_ref[...]

    ones_wide = _head_block_ones(MXU_WIDTH)
    ones_narrow = _head_block_ones(w - MXU_WIDTH)
    tri_incl = jnp.where(lax.broadcasted_iota(jnp.int32, (t, t), 1) <= lax.broadcasted_iota(jnp.int32, (t, t), 0),
                         1.0, 0.0).astype(BF16)
    sub_row = lax.broadcasted_iota(jnp.int32, (nsb, sub, w), 1)
    if nsb > 1:
        n_stack = sub * (nsb * (nsb - 1) // 2)
        seg_start = [sub * (i * (i - 1) // 2) for i in range(nsb + 1)]
        tr = lax.broadcasted_iota(jnp.int32, (t, n_stack), 0) // sub
        tc = lax.broadcasted_iota(jnp.int32, (t, n_stack), 1)
        pair = jnp.zeros((t, n_stack), jnp.bool_)
        for i in range(1, nsb):
            pair = pair | ((tr == i) & (tc >= seg_start[i]) & (tc < seg_start[i + 1]))

    def group(gi, carry):
        chunks = range(n_group)
        rows = [pl.ds(pl.multiple_of((gi * n_group + j) * t, t), t) for j in chunks]
        q = [qs_ref[0, r, :] for r in rows]
        k = [kk_ref[0, r, :] for r in rows]
        v = [hv_ref[0, r, :] for r in rows]

        b = []
        for r in rows:
            lf_hi, lf_lo = _split_hi_lo(lf_ref[0, r, :])
            b.append(_dot(tri_incl, lf_hi) + _dot(tri_incl, lf_lo))

        q_in, k_st, decay = [], [], []
        for j in chunks:
            b_last = b[j][t - 1:t, :]
            q_in.append((q[j] * jnp.exp(b[j])).astype(BF16))
            k_st.append((k[j] * jnp.exp(b_last - b[j])).astype(BF16))
            decay.append(jnp.exp(b_last))

        o_off = [None] * n_group
        if nsb > 1:
            qt, kt, vst = [], [], []
            for j in chunks:
                qt_parts = [jnp.zeros((sub, w), F32)]
                kt_parts = []
                vs_parts = []
                for i in range(1, nsb):
                    ref_row = b[j][i * sub - 1:i * sub, :]
                    qt_parts.append(q[j][i * sub:(i + 1) * sub, :]
                                    * jnp.exp(b[j][i * sub:(i + 1) * sub, :] - ref_row))
                    kt_parts.append(k[j][:i * sub, :] * jnp.exp(ref_row - b[j][:i * sub, :]))
                    vs_parts.append(v[j][:i * sub, :])
                qt.append(jnp.concatenate(qt_parts, axis=0).astype(BF16))
                kt.append(jnp.concatenate(kt_parts, axis=0).astype(BF16))
                vst.append(jnp.concatenate(vs_parts, axis=0).astype(BF16))
            sc = [[_dot_nt(qt[j][:, lanes], kt[j][:, lanes]) for lanes in heads] for j in chunks]
            sc = [[jnp.where(pair, x, 0.0).astype(BF16) for x in row] for row in sc]
            o_off = [jnp.concatenate([_dot(sc[j][h], vst[j][:, lanes]) for h, lanes in enumerate(heads)], axis=1)
                     for j in chunks]

        upd = [[_dot(jnp.transpose(v[j][:, lanes]).astype(BF16), k_st[j][:, lanes]) for lanes in heads]
               for j in chunks]
        states = [[st_ref[0, h] for h in range(HG_HEADS)]]
        for j in chunks:
            states.append([states[j][h] * decay[j][:, lanes] + upd[j][h] for h, lanes in enumerate(heads)])
        for h in range(HG_HEADS):
            st_ref[0, h] = states[n_group][h]
        o_state = [jnp.concatenate([_dot_nt(q_in[j][:, lanes], states[j][h].astype(BF16))
                                    for h, lanes in enumerate(heads)], axis=1) for j in chunks]

        p_all = []
        for j in chunks:
            b3 = b[j].reshape(nsb, sub, w)
            q3 = q[j].reshape(nsb, sub, w)
            k3 = k[j].reshape(nsb, sub, w)
            terms = []
            for s in range(sub):
                bs = jnp.broadcast_to(b3[:, s:s + 1, :], (nsb, sub, w))
                ks = jnp.broadcast_to(k3[:, s:s + 1, :], (nsb, sub, w))
                e = jnp.where(sub_row >= s, jnp.exp(b3 - bs), 0.0)
                terms.append((q3 * ks * e).reshape(nsb * sub, w))
            a_all = jnp.concatenate(terms, axis=0).astype(BF16)
            p_all.append(_head_sums(a_all, ones_wide, ones_narrow))

        for j in chunks:
            v3 = v[j].reshape(nsb, sub, w)
            o = jnp.zeros((nsb, sub, w), F32)
            for s in range(sub):
                vs = jnp.broadcast_to(v3[:, s:s + 1, :], (nsb, sub, w))
                o = o + p_all[j][s * t:(s + 1) * t, :].reshape(nsb, sub, w) * vs
            o = o.reshape(t, w) + o_state[j]
            if nsb > 1:
                o = o + o_off[j]
            sq_hi, sq_lo = _split_hi_lo(o * o)
            ms = (_head_sums(sq_hi, ones_wide, ones_narrow)
                  + _head_sums(sq_lo, ones_wide, ones_narrow)) * (1.0 / HEAD_DIM)
            o_ref[0, rows[j], :] = (o * lax.rsqrt(ms + RMS_EPS) * nw_ref[...]
                                    * gt_ref[0, rows[j], :]).astype(BF16)
        return carry

    lax.fori_loop(0, rb // (t * n_group), group, 0)


def _hgrn_call(qs, kk, hv, lf, gt, norm_w, s0_t, *, t):
    b, l, w = qs.shape
    rb = min(512, l)
    n_group = min(HG_GROUP, rb // t)
    assert l % rb == 0 and rb % (t * n_group) == 0 and t % HG_SUB == 0
    row_spec = pl.BlockSpec((1, rb, w), lambda i, j: (i, j, 0))
    st_spec = pl.BlockSpec((1, HG_HEADS, HEAD_DIM, HEAD_DIM), lambda i, j: (i, 0, 0, 0))
    return pl.pallas_call(
        functools.partial(_hgrn_kernel, rb=rb, t=t, n_group=n_group),
        grid=(b, l // rb),
        in_specs=[row_spec] * 5 + [pl.BlockSpec((1, w), lambda i, j: (0, 0)), st_spec],
        out_specs=[row_spec, st_spec],
        out_shape=[jax.ShapeDtypeStruct((b, l, w), BF16),
                   jax.ShapeDtypeStruct((b, HG_HEADS, HEAD_DIM, HEAD_DIM), F32)],
        compiler_params=pltpu.CompilerParams(dimension_semantics=("parallel", "arbitrary"),
                                             vmem_limit_bytes=VMEM_LIMIT_BYTES),
        name="hgrn",
    )(qs, kk, hv, lf, gt, norm_w, s0_t)


def _ffn_kernel(x_ref, oa_ref, ob_ref, oc_ref, wo_ref, g1_ref, b1_ref, w1_ref, w2_ref, g2_ref, b2_ref,
                y_ref, *, alpha, ff_chunk):
    mix = _dot(jnp.concatenate([oa_ref[...], ob_ref[...], oc_ref[...]], axis=1), wo_ref[...])
    x1 = _layer_norm(alpha * x_ref[...] + mix, g1_ref[...], b1_ref[...])
    x1b = x1.astype(BF16)
    n_chunks = w1_ref.shape[1] // ff_chunk
    cols = [slice(c * ff_chunk, (c + 1) * ff_chunk) for c in range(n_chunks)]

    def hidden(c):
        return jnp.maximum(_dot(x1b, w1_ref[:, cols[c]]), 0.0)

    ff = None
    hdn = hidden(0)
    for c in range(n_chunks):
        hdn_next = hidden(c + 1) if c + 1 < n_chunks else None
        part = _dot((hdn * hdn).astype(BF16), w2_ref[cols[c], :])
        ff = part if ff is None else ff + part
        hdn = hdn_next
    y_ref[...] = _layer_norm(alpha * x1 + ff, g2_ref[...], b2_ref[...])


def _ffn_call(x2d, oa, ob, oc, wo, g1, b1, w1, w2, g2, b2, *, alpha):
    n, d = x2d.shape
    d_ff = w1.shape[1]
    tm = min(512, n)
    ff_chunk = min(1024, d_ff)
    assert n % tm == 0 and d_ff % ff_chunk == 0
    row_spec = lambda width: pl.BlockSpec((tm, width), lambda i: (i, 0))
    const = lambda shape: pl.BlockSpec(shape, lambda i: (0,) * len(shape), pipeline_mode=pl.Buffered(1))
    return pl.pallas_call(
        functools.partial(_ffn_kernel, alpha=alpha, ff_chunk=ff_chunk),
        grid=(n // tm,),
        in_specs=[row_spec(d), row_spec(SB_WIDTH), row_spec(MLP_WIDTH), row_spec(HG_WIDTH),
                  const(wo.shape), const((1, d)), const((1, d)),
                  const(w1.shape), const(w2.shape), const((1, d)), const((1, d))],
        out_specs=row_spec(d),
        out_shape=jax.ShapeDtypeStruct((n, d), F32),
        compiler_params=pltpu.CompilerParams(dimension_semantics=("parallel",),
                                             vmem_limit_bytes=VMEM_LIMIT_BYTES),
        name="ffn",
    )(x2d, oa, ob, oc, wo, g1, b1, w1, w2, g2, b2)


def _trunk(x, past_k, past_v, hg_state, w_in, w_out, mlp_ln_g, mlp_ln_b, mlp_ws, mlp_bs,
           hg_lb_logits, hg_norm_w, ln1_g, ln1_b, w_ff1, w_ff2, ln2_g, ln2_b):
    bsz, l, d = x.shape
    depth = w_in.shape[0]
    n = bsz * l
    alpha = (2 * depth) ** 0.25
    prompt = past_k is None
    cl = MLP_CHUNK if prompt else l
    assert MLP_CHUNK % cl == 0 and l % cl == 0
    reps = MLP_CHUNK // cl

    g = jax.nn.softmax(hg_lb_logits.astype(F32), axis=0)
    cs = jnp.cumsum(g, axis=0)
    lbs = cs - cs[0:1]

    x2d = x.reshape(n, d)
    carried_kv = ()
    new_s, new_mv = [], []
    for layer in range(depth):
        lb = lbs[layer][None, :]
        wm_tiled = jnp.tile(mlp_ws[layer][:, :cl, :cl], (1, reps, reps))
        brow = jnp.tile(jnp.repeat(jnp.transpose(mlp_bs[layer][:, :cl]), HEAD_DIM, axis=1), (reps, 1))
        outs = _proj_call(x2d, w_in[layer].astype(BF16), mlp_ln_g[layer][None], mlp_ln_b[layer][None],
                          wm_tiled, brow, jnp.log(lb), jnp.log1p(-lb), 1.0 - lb, carried_kv,
                          layer=layer, depth=depth, seq_len=l, cl=cl, emit_vn=not prompt)
        carried_kv = tuple(outs[:2])
        qa, kab, vab, ob, qs, kk, hv, lf, gt = outs[2:11]
        shp = lambda a: a.reshape(bsz, l, a.shape[-1])
        if prompt:
            oa = _attn_call(shp(qa), shp(kab), shp(vab), q_start=0)
            s0_t = jnp.zeros((bsz, HG_HEADS, HEAD_DIM, HEAD_DIM), F32)
            t = HG_CHUNK
        else:
            past_len = past_k.shape[2]
            pad = (-(past_len + l)) % KEY_BLOCK
            cat = lambda past, new: jnp.pad(
                jnp.concatenate([past.reshape(bsz, past_len, SB_WIDTH).astype(BF16), shp(new)], axis=1),
                ((0, 0), (0, pad), (0, 0)))
            oa = _attn_call(shp(qa), cat(past_k[layer], kab), cat(past_v[layer], vab), q_start=past_len)
            s0_t = jnp.swapaxes(hg_state[layer].astype(F32), -1, -2)
            t = l
            new_mv.append(outs[11].reshape(bsz, l, MLP_WIDTH))
        oc, s_t = _hgrn_call(shp(qs), shp(kk), shp(hv), shp(lf), shp(gt), hg_norm_w[layer][None], s0_t, t=t)
        x2d = _ffn_call(x2d, oa.reshape(n, SB_WIDTH), ob, oc.reshape(n, HG_WIDTH),
                        w_out[layer].astype(BF16), ln1_g[layer][None], ln1_b[layer][None],
                        w_ff1[layer].astype(BF16), w_ff2[layer].astype(BF16),
                        ln2_g[layer][None], ln2_b[layer][None], alpha=alpha)
        new_s.append(jnp.swapaxes(s_t, -1, -2))
    mv = None if prompt else jnp.stack(new_mv)
    if carried_kv[0].ndim == 5:
        new_k, new_v = (jnp.transpose(a, (0, 1, 4, 2, 3)) for a in carried_kv)
    else:
        new_k, new_v = (a.reshape(depth, bsz, l, SB_HEADS, HEAD_DIM) for a in carried_kv)
    return x2d.reshape(bsz, l, d), new_k, new_v, jnp.stack(new_s), mv


def kernel(x_prompt, x_sample, cache_sb_k, cache_sb_v, state_hgrn, w_in, w_out, mlp_ln_g, mlp_ln_b, mlp_ws, mlp_bs, hg_lb_logits, hg_norm_w, ln1_g, ln1_b, w_ff1, w_ff2, ln2_g, ln2_b):
    weights = (w_in, w_out, mlp_ln_g, mlp_ln_b, mlp_ws, mlp_bs, hg_lb_logits, hg_norm_w,
               ln1_g, ln1_b, w_ff1, w_ff2, ln2_g, ln2_b)
    y_p, k_p, v_p, s_p, _ = _trunk(x_prompt, None, None, None, *weights)
    y_s, k_s, v_s, s_s, mv_s = _trunk(x_sample, cache_sb_k, cache_sb_v, state_hgrn, *weights)
    return (y_p, y_s, k_p, v_p, s_p, k_s, v_s, s_s, mv_s)
```

```python
import functools
import math

import jax
import jax.numpy as jnp
from jax import lax
from jax.experimental import pallas as pl
from jax.experimental.pallas import tpu as pltpu

F32 = jnp.float32
BF16 = jnp.bfloat16

HEAD_DIM = 64
SB_HEADS = 6
SB_WIDTH = SB_HEADS * HEAD_DIM
MLP_GROUPS = 4
MLP_WIDTH = MLP_GROUPS * HEAD_DIM
MLP_CHUNK = 128
HG_HEADS = 6
HG_WIDTH = HG_HEADS * HEAD_DIM
HG_CHUNK = 64
HG_SUB = 8
HG_GROUP = 8
MXU_WIDTH = 256
KEY_BLOCK = 128
SUPER_KEYS = 256
UNDERFLOW_LOG2 = -160.0
LOG2E = 1.4426950408889634
LN_EPS = 1e-5
RMS_EPS = 1e-6

OFF_QA = 0
OFF_KA = OFF_QA + SB_WIDTH
OFF_VA = OFF_KA + SB_WIDTH
OFF_UB = OFF_VA + SB_WIDTH
OFF_VB = OFF_UB + MLP_WIDTH
OFF_QC = OFF_VB + MLP_WIDTH
OFF_FC = OFF_QC + HG_WIDTH
OFF_IC = OFF_FC + HG_WIDTH
OFF_GC = OFF_IC + HG_WIDTH
IN_WIDTH = OFF_GC + HG_WIDTH

V7X_VMEM_BYTES = 64 * 1024 * 1024
VMEM_LIMIT_BYTES = V7X_VMEM_BYTES - 8 * 1024 * 1024


def _dot(a, b):
    return jnp.dot(a, b, preferred_element_type=F32)


def _dot_nt(a, b):
    return lax.dot_general(a, b, (((1,), (1,)), ((), ())), preferred_element_type=F32)


def _sigmoid(x):
    return 1.0 / (1.0 + jnp.exp(-x))


def _log_sigmoid(x):
    return jnp.minimum(x, 0.0) - jnp.log1p(jnp.exp(-jnp.abs(x)))


def _layer_norm(x, g, b):
    mu = jnp.mean(x, axis=-1, keepdims=True)
    xc = x - mu
    var = jnp.mean(xc * xc, axis=-1, keepdims=True)
    return xc * lax.rsqrt(var + LN_EPS) * g + b


def _split_hi_lo(x):
    hi = x.astype(BF16)
    lo = (x - hi.astype(F32)).astype(BF16)
    return hi, lo


def _head_block_ones(width):
    r = lax.broadcasted_iota(jnp.int32, (width, width), 0) // HEAD_DIM
    c = lax.broadcasted_iota(jnp.int32, (width, width), 1) // HEAD_DIM
    return jnp.where(r == c, 1.0, 0.0).astype(BF16)


def _head_sums(x, ones_wide, ones_narrow):
    split = ones_wide.shape[0]
    return jnp.concatenate([_dot(x[:, :split], ones_wide), _dot(x[:, split:], ones_narrow)], axis=1)


def _layer_weight_spec(shape, layer):
    return pl.BlockSpec((None,) + tuple(shape[1:]), lambda i: (layer,) + (0,) * (len(shape) - 1),
                        pipeline_mode=pl.Buffered(1))


def _const_spec(shape):
    return pl.BlockSpec(shape, lambda i: (0,) * len(shape), pipeline_mode=pl.Buffered(1))


def _proj_kernel(x_ref, w_ref, lng_ref, lnb_ref, wm_ref, brow_ref, loglb_ref, l1mlb_ref, omlb_ref,
                 *rest, tm, cl, n_carried, layer, depth, head_major):
    (ka_ref, va_ref, qa_ref, kab_ref, vab_ref, ob_ref, qs_ref, kk_ref, hv_ref, lf_ref,
     gt_ref, *maybe_vn_ref) = rest[n_carried:]
    xb = x_ref[...].astype(BF16)

    n_tiles = pl.cdiv(IN_WIDTH, MXU_WIDTH)
    first = OFF_QC // MXU_WIDTH
    tiles = {}
    for i in list(range(first, n_tiles)) + list(range(first)):
        tiles[i] = _dot(xb, w_ref[:, i * MXU_WIDTH:min((i + 1) * MXU_WIDTH, IN_WIDTH)])

    def proj(off, width):
        parts = []
        pos = off
        while pos < off + width:
            i, lo = divmod(pos, MXU_WIDTH)
            take = min(MXU_WIDTH - lo, off + width - pos)
            parts.append(tiles[i][:, lo:lo + take])
            pos += take
        return parts[0] if len(parts) == 1 else jnp.concatenate(parts, axis=1)

    qc = proj(OFF_QC, HG_WIDTH)
    qs_ref[...] = qc * _sigmoid(qc)
    zf = proj(OFF_FC, HG_WIDTH)
    c_term = l1mlb_ref[...] + _log_sigmoid(zf)
    a_term = loglb_ref[...]
    lf_ref[...] = jnp.maximum(a_term, c_term) + jnp.log1p(jnp.exp(-jnp.abs(a_term - c_term)))
    kk_ref[...] = omlb_ref[...] * _sigmoid(-zf)
    hv_ref[...] = proj(OFF_IC, HG_WIDTH)
    gt_ref[...] = _sigmoid(proj(OFF_GC, HG_WIDTH))

    vn = _layer_norm(proj(OFF_VB, MLP_WIDTH), lng_ref[...], lnb_ref[...])
    if maybe_vn_ref:
        maybe_vn_ref[0][...] = vn
    ub = proj(OFF_UB, MLP_WIDTH)
    row = lax.broadcasted_iota(jnp.int32, (MLP_CHUNK, MLP_CHUNK), 0)
    col = lax.broadcasted_iota(jnp.int32, (MLP_CHUNK, MLP_CHUNK), 1)
    keep = (col <= row) & ((row // cl) == (col // cl))
    wm = [jnp.where(keep, wm_ref[g], 0.0).astype(BF16) for g in range(MLP_GROUPS)]
    left_head = lax.broadcasted_iota(jnp.int32, (MLP_CHUNK, 2 * HEAD_DIM), 1) < HEAD_DIM
    brow = brow_ref[...]
    vnb = vn.astype(BF16)
    for c in range(tm // MLP_CHUNK):
        rows = slice(c * MLP_CHUNK, (c + 1) * MLP_CHUNK)
        for p in range(MLP_GROUPS // 2):
            lanes = slice(p * 2 * HEAD_DIM, (p + 1) * 2 * HEAD_DIM)
            vp = vnb[rows, lanes]
            mixed = jnp.where(left_head, _dot(wm[2 * p], vp), _dot(wm[2 * p + 1], vp)) + brow[:, lanes]
            ob_ref[rows, lanes] = (ub[rows, lanes] * mixed).astype(BF16)

    qa_ref[...] = (proj(OFF_QA, SB_WIDTH) * (LOG2E / math.sqrt(HEAD_DIM))).astype(BF16)
    ka = proj(OFF_KA, SB_WIDTH)
    kab_ref[...] = ka.astype(BF16)
    va = proj(OFF_VA, SB_WIDTH)
    vab_ref[...] = va.astype(BF16)
    for src, dst in ((ka, ka_ref), (va, va_ref)):
        if n_carried == 0:
            for other in range(depth):
                if other != layer:
                    dst[other] = jnp.zeros(dst.shape[1:], F32)
            dst = dst.at[layer]
        if head_major:
            for p in range(SB_HEADS // 2):
                pair_t = jnp.transpose(src[:, p * 2 * HEAD_DIM:(p + 1) * 2 * HEAD_DIM])
                dst[2 * p] = pair_t[:HEAD_DIM]
                dst[2 * p + 1] = pair_t[HEAD_DIM:]
        else:
            dst[...] = src


def _proj_call(x2d, w_in_bf, ln_g, ln_b, wm_tiled, brow, loglb, l1mlb, omlb, carried_kv, *,
               layer, depth, seq_len, cl, emit_vn):
    n, d = x2d.shape
    tm = min(512, n)
    assert n % tm == 0 and tm % MLP_CHUNK == 0 and MLP_CHUNK % cl == 0
    head_major = seq_len % tm == 0
    first = not carried_kv
    row_spec = lambda width: pl.BlockSpec((tm, width), lambda i: (i, 0))
    layer_dim = depth if first else None
    layer_idx = 0 if first else layer
    if head_major:
        per_seq = seq_len // tm
        kv_shape = (depth, n // seq_len, SB_HEADS, HEAD_DIM, seq_len)
        layer_spec = pl.BlockSpec((layer_dim, None, SB_HEADS, HEAD_DIM, tm),
                                  lambda i: (layer_idx, i // per_seq, 0, 0, i % per_seq))
    else:
        kv_shape = (depth, n, SB_WIDTH)
        layer_spec = pl.BlockSpec((layer_dim, tm, SB_WIDTH), lambda i: (layer_idx, i, 0))
    out_shapes = [
        jax.ShapeDtypeStruct(kv_shape, F32),
        jax.ShapeDtypeStruct(kv_shape, F32),
        jax.ShapeDtypeStruct((n, SB_WIDTH), BF16),
        jax.ShapeDtypeStruct((n, SB_WIDTH), BF16),
        jax.ShapeDtypeStruct((n, SB_WIDTH), BF16),
        jax.ShapeDtypeStruct((n, MLP_WIDTH), BF16),
        jax.ShapeDtypeStruct((n, HG_WIDTH), F32),
        jax.ShapeDtypeStruct((n, HG_WIDTH), F32),
        jax.ShapeDtypeStruct((n, HG_WIDTH), F32),
        jax.ShapeDtypeStruct((n, HG_WIDTH), F32),
        jax.ShapeDtypeStruct((n, HG_WIDTH), F32),
    ]
    if emit_vn:
        out_shapes.append(jax.ShapeDtypeStruct((n, MLP_WIDTH), F32))
    out_specs = [layer_spec, layer_spec] + [row_spec(s.shape[1]) for s in out_shapes[2:]]
    n_fixed = 9
    return pl.pallas_call(
        functools.partial(_proj_kernel, tm=tm, cl=cl, n_carried=len(carried_kv), layer=layer, depth=depth,
                          head_major=head_major),
        grid=(n // tm,),
        in_specs=[
            row_spec(d),
            _layer_weight_spec(w_in_bf.shape, layer),
            _const_spec((1, MLP_WIDTH)), _const_spec((1, MLP_WIDTH)),
            _const_spec((MLP_GROUPS, MLP_CHUNK, MLP_CHUNK)),
            _const_spec((MLP_CHUNK, MLP_WIDTH)),
            _const_spec((1, HG_WIDTH)), _const_spec((1, HG_WIDTH)), _const_spec((1, HG_WIDTH)),
        ] + [pl.BlockSpec(memory_space=pl.ANY)] * len(carried_kv),
        out_specs=out_specs,
        out_shape=out_shapes,
        input_output_aliases={n_fixed + j: j for j in range(len(carried_kv))},
        compiler_params=pltpu.CompilerParams(dimension_semantics=("parallel",),
                                             vmem_limit_bytes=VMEM_LIMIT_BYTES),
        name="proj",
    )(x2d, w_in_bf, ln_g, ln_b, wm_tiled, brow, loglb, l1mlb, omlb, *carried_kv)


def _sb_scores(z, mask, uu):
    kb = KEY_BLOCK
    log_beta = jnp.minimum(z, 0.0) - jnp.log2(1.0 + jnp.exp2(-jnp.abs(z)))
    log_keep = log_beta - z
    if mask is not None:
        log_keep = jnp.where(mask, log_keep, 0.0)
    hi, lo = _split_hi_lo(log_keep)
    sums = _dot(jnp.concatenate([hi, lo], axis=1), uu)
    return log_beta + sums[:, :kb], sums[:, kb:]


def _sb_finish(pre, total, mask, run):
    w = jnp.exp2(pre + run)
    if mask is not None:
        w = jnp.where(mask, w, 0.0)
    return w.astype(BF16), run + total


def _attn_kernel(q_ref, k_ref, v_ref, o_ref, run_ref, acc_ref, *, tq, nstrip, nsuper, q_start):
    kb = KEY_BLOCK
    r = lax.broadcasted_iota(jnp.int32, (2 * kb, 2 * kb), 0) % kb
    c = lax.broadcasted_iota(jnp.int32, (2 * kb, 2 * kb), 1)
    uu = jnp.where((c >= kb) | (r > c), 1.0, 0.0).astype(BF16)
    n_rows = nstrip * tq
    row = lax.broadcasted_iota(jnp.int32, (n_rows, kb), 0)
    col = lax.broadcasted_iota(jnp.int32, (n_rows, kb), 1)
    own_mask = (row >= tq) | (col < row)
    heads = [slice(h * HEAD_DIM, (h + 1) * HEAD_DIM) for h in range(SB_HEADS)]
    n_before = q_start // SUPER_KEYS

    nblk = SUPER_KEYS // kb

    def aligned(x, m):
        return x if isinstance(x, int) else pl.multiple_of(x, m)

    def q_super(qi, has_earlier):
        q0 = aligned(qi * n_rows, n_rows)
        n_earlier = n_before + qi
        kd0 = aligned(n_earlier * SUPER_KEYS, SUPER_KEYS)
        run_ref[...] = jnp.zeros_like(run_ref)
        acc_ref[...] = jnp.zeros_like(acc_ref)

        def own_logits(g):
            lo = g * tq
            return [_dot_nt(q_ref[0, pl.ds(q0 + lo, n_rows - lo), lanes],
                            k_ref[0, pl.ds(kd0 + g * kb, kb), lanes]) for lanes in heads]

        def own_scores(g, logits):
            return [_sb_scores(z, own_mask[:n_rows - g * tq], uu) for z in logits]

        def own_finish(g, scores):
            lo = g * tq
            for h, lanes in enumerate(heads):
                pre, total = scores[h]
                w, run = _sb_finish(pre, total, own_mask[:n_rows - lo], run_ref[h, lo:, :])
                run_ref[h, lo:, :] = run
                acc_ref[h, lo:, :] += _dot(w, v_ref[0, pl.ds(kd0 + g * kb, kb), lanes])

        def earlier_logits(k0):
            return [_dot_nt(q_ref[0, pl.ds(q0, n_rows), lanes], k_ref[0, pl.ds(k0, SUPER_KEYS), lanes])
                    for lanes in heads]

        def earlier_scores(logits):
            return [[_sb_scores(z[:, g * kb:(g + 1) * kb], None, uu) for g in range(nblk)] for z in logits]

        def earlier_finish(k0, scores):
            for h, lanes in enumerate(heads):
                run = run_ref[h]
                ws = [None] * nblk
                for g in reversed(range(nblk)):
                    ws[g], run = _sb_finish(*scores[h][g], None, run)
                run_ref[h] = run
                acc_ref[h] += _dot(jnp.concatenate(ws, axis=1), v_ref[0, pl.ds(k0, SUPER_KEYS), lanes])

        k_prev = pl.multiple_of((n_earlier - 1) * SUPER_KEYS, SUPER_KEYS) if has_earlier else None
        own = [own_logits(g) for g in range(nstrip)]
        prev = earlier_logits(k_prev) if has_earlier else None
        own = [own_scores(g, own[g]) for g in range(nstrip)]
        prev = earlier_scores(prev) if has_earlier else None
        for g in reversed(range(nstrip)):
            own_finish(g, own[g])
        if has_earlier:
            earlier_finish(k_prev, prev)

            def alive(c):
                return (c[0] < n_earlier) & (c[1] > UNDERFLOW_LOG2)

            def k_super(c):
                k0 = pl.multiple_of((n_earlier - 1 - c[0]) * SUPER_KEYS, SUPER_KEYS)
                earlier_finish(k0, earlier_scores(earlier_logits(k0)))
                return c[0] + 1, jnp.max(run_ref[...])

            lax.while_loop(alive, k_super, (jnp.int32(1), jnp.max(run_ref[...])))
        o_ref[0, pl.ds(q0, n_rows), :] = jnp.concatenate(
            [acc_ref[h] for h in range(SB_HEADS)], axis=1).astype(BF16)

    def q_super_step(qi, carry):
        q_super(qi, True)
        return carry

    if n_before == 0:
        q_super(0, False)
        lax.fori_loop(1, nsuper, q_super_step, 0)
    else:
        lax.fori_loop(0, nsuper, q_super_step, 0)


def _attn_call(q, k, v, *, q_start):
    b, lq, _ = q.shape
    lk = k.shape[1]
    if lq % SUPER_KEYS == 0:
        tq, nstrip, nsuper = KEY_BLOCK, SUPER_KEYS // KEY_BLOCK, lq // SUPER_KEYS
    else:
        tq, nstrip, nsuper = lq, 1, 1
    assert tq <= KEY_BLOCK and q_start % SUPER_KEYS == 0 and q_start + lq <= lk and lk % KEY_BLOCK == 0
    spec = lambda l: pl.BlockSpec((1, l, SB_WIDTH), lambda i: (i, 0, 0))
    return pl.pallas_call(
        functools.partial(_attn_kernel, tq=tq, nstrip=nstrip, nsuper=nsuper, q_start=q_start),
        grid=(b,),
        in_specs=[spec(lq), spec(lk), spec(lk)],
        out_specs=spec(lq),
        out_shape=jax.ShapeDtypeStruct((b, lq, SB_WIDTH), BF16),
        scratch_shapes=[pltpu.VMEM((SB_HEADS, nstrip * tq, KEY_BLOCK), F32),
                        pltpu.VMEM((SB_HEADS, nstrip * tq, HEAD_DIM), F32)],
        compiler_params=pltpu.CompilerParams(dimension_semantics=("parallel",),
                                             vmem_limit_bytes=VMEM_LIMIT_BYTES),
        name="attn",
    )(q, k, v)


def _hgrn_kernel(qs_ref, kk_ref, hv_ref, lf_ref, gt_ref, nw_ref, s0_ref, o_ref, st_ref, *, rb, t, n_group):
    nsb = t // HG_SUB
    sub = HG_SUB
    w = HG_WIDTH
    heads = [slice(h * HEAD_DIM, (h + 1) * HEAD_DIM) for h in range(HG_HEADS)]

    @pl.when(pl.program_id(1) == 0)
    def _():
        st_ref[...] = s0_ref[...]

    ones_wide = _head_block_ones(MXU_WIDTH)
    ones_narrow = _head_block_ones(w - MXU_WIDTH)
    tri_incl = jnp.where(lax.broadcasted_iota(jnp.int32, (t, t), 1) <= lax.broadcasted_iota(jnp.int32, (t, t), 0),
                         1.0, 0.0).astype(BF16)
    sub_row = lax.broadcasted_iota(jnp.int32, (nsb, sub, w), 1)
    if nsb > 1:
        n_stack = sub * (nsb * (nsb - 1) // 2)
        seg_start = [sub * (i * (i - 1) // 2) for i in range(nsb + 1)]
        tr = lax.broadcasted_iota(jnp.int32, (t, n_stack), 0) // sub
        tc = lax.broadcasted_iota(jnp.int32, (t, n_stack), 1)
        pair = jnp.zeros((t, n_stack), jnp.bool_)
        for i in range(1, nsb):
            pair = pair | ((tr == i) & (tc >= seg_start[i]) & (tc < seg_start[i + 1]))

    def group(gi, carry):
        chunks = range(n_group)
        rows = [pl.ds(pl.multiple_of((gi * n_group + j) * t, t), t) for j in chunks]
        q = [qs_ref[0, r, :] for r in rows]
        k = [kk_ref[0, r, :] for r in rows]
        v = [hv_ref[0, r, :] for r in rows]

        b = []
        for r in rows:
            lf_hi, lf_lo = _split_hi_lo(lf_ref[0, r, :])
            b.append(_dot(tri_incl, lf_hi) + _dot(tri_incl, lf_lo))

        q_in, k_st, decay = [], [], []
        for j in chunks:
            b_last = b[j][t - 1:t, :]
            q_in.append((q[j] * jnp.exp(b[j])).astype(BF16))
            k_st.append((k[j] * jnp.exp(b_last - b[j])).astype(BF16))
            decay.append(jnp.exp(b_last))

        o_off = [None] * n_group
        if nsb > 1:
            qt, kt, vst = [], [], []
            for j in chunks:
                qt_parts = [jnp.zeros((sub, w), F32)]
                kt_parts = []
                vs_parts = []
                for i in range(1, nsb):
                    ref_row = b[j][i * sub - 1:i * sub, :]
                    qt_parts.append(q[j][i * sub:(i + 1) * sub, :]
                                    * jnp.exp(b[j][i * sub:(i + 1) * sub, :] - ref_row))
                    kt_parts.append(k[j][:i * sub, :] * jnp.exp(ref_row - b[j][:i * sub, :]))
                    vs_parts.append(v[j][:i * sub, :])
                qt.append(jnp.concatenate(qt_parts, axis=0).astype(BF16))
                kt.append(jnp.concatenate(kt_parts, axis=0).astype(BF16))
                vst.append(jnp.concatenate(vs_parts, axis=0).astype(BF16))
            sc = [[_dot_nt(qt[j][:, lanes], kt[j][:, lanes]) for lanes in heads] for j in chunks]
            sc = [[jnp.where(pair, x, 0.0).astype(BF16) for x in row] for row in sc]
            o_off = [jnp.concatenate([_dot(sc[j][h], vst[j][:, lanes]) for h, lanes in enumerate(heads)], axis=1)
                     for j in chunks]

        upd = [[_dot(jnp.transpose(v[j][:, lanes]).astype(BF16), k_st[j][:, lanes]) for lanes in heads]
               for j in chunks]
        states = [[st_ref[0, h] for h in range(HG_HEADS)]]
        for j in chunks:
            states.append([states[j][h] * decay[j][:, lanes] + upd[j][h] for h, lanes in enumerate(heads)])
        for h in range(HG_HEADS):
            st_ref[0, h] = states[n_group][h]
        o_state = [jnp.concatenate([_dot_nt(q_in[j][:, lanes], states[j][h].astype(BF16))
                                    for h, lanes in enumerate(heads)], axis=1) for j in chunks]

        p_all = []
        for j in chunks:
            b3 = b[j].reshape(nsb, sub, w)
            q3 = q[j].reshape(nsb, sub, w)
            k3 = k[j].reshape(nsb, sub, w)
            terms = []
            for s in range(sub):
                bs = jnp.broadcast_to(b3[:, s:s + 1, :], (nsb, sub, w))
                ks = jnp.broadcast_to(k3[:, s:s + 1, :], (nsb, sub, w))
                e = jnp.exp(b3 - bs)
                if s > 0:
                    e = jnp.where(sub_row >= s, e, 0.0)
                terms.append((q3 * ks * e).reshape(nsb * sub, w))
            a_all = jnp.concatenate(terms, axis=0).astype(BF16)
            p_all.append(_head_sums(a_all, ones_wide, ones_narrow))

        for j in chunks:
            v3 = v[j].reshape(nsb, sub, w)
            o = jnp.zeros((nsb, sub, w), F32)
            for s in range(sub):
                vs = jnp.broadcast_to(v3[:, s:s + 1, :], (nsb, sub, w))
                o = o + p_all[j][s * t:(s + 1) * t, :].reshape(nsb, sub, w) * vs
            o = o.reshape(t, w) + o_state[j]
            if nsb > 1:
                o = o + o_off[j]
            sq_hi, sq_lo = _split_hi_lo(o * o)
            ms = (_head_sums(sq_hi, ones_wide, ones_narrow)
                  + _head_sums(sq_lo, ones_wide, ones_narrow)) * (1.0 / HEAD_DIM)
            o_ref[0, rows[j], :] = (o * lax.rsqrt(ms + RMS_EPS) * nw_ref[...]
                                    * gt_ref[0, rows[j], :]).astype(BF16)
        return carry

    lax.fori_loop(0, rb // (t * n_group), group, 0)


def _hgrn_call(qs, kk, hv, lf, gt, norm_w, s0_t, *, t):
    b, l, w = qs.shape
    rb = min(512, l)
    n_group = min(HG_GROUP, rb // t)
    assert l % rb == 0 and rb % (t * n_group) == 0 and t % HG_SUB == 0
    row_spec = pl.BlockSpec((1, rb, w), lambda i, j: (i, j, 0))
    st_spec = pl.BlockSpec((1, HG_HEADS, HEAD_DIM, HEAD_DIM), lambda i, j: (i, 0, 0, 0))
    return pl.pallas_call(
        functools.partial(_hgrn_kernel, rb=rb, t=t, n_group=n_group),
        grid=(b, l // rb),
        in_specs=[row_spec] * 5 + [pl.BlockSpec((1, w), lambda i, j: (0, 0)), st_spec],
        out_specs=[row_spec, st_spec],
        out_shape=[jax.ShapeDtypeStruct((b, l, w), BF16),
                   jax.ShapeDtypeStruct((b, HG_HEADS, HEAD_DIM, HEAD_DIM), F32)],
        compiler_params=pltpu.CompilerParams(dimension_semantics=("parallel", "arbitrary"),
                                             vmem_limit_bytes=VMEM_LIMIT_BYTES),
        name="hgrn",
    )(qs, kk, hv, lf, gt, norm_w, s0_t)


def _ffn_kernel(x_ref, oa_ref, ob_ref, oc_ref, wo_ref, g1_ref, b1_ref, w1_ref, w2_ref, g2_ref, b2_ref,
                y_ref, *, alpha, ff_chunk):
    mix = _dot(jnp.concatenate([oa_ref[...], ob_ref[...], oc_ref[...]], axis=1), wo_ref[...])
    x1 = _layer_norm(alpha * x_ref[...] + mix, g1_ref[...], b1_ref[...])
    x1b = x1.astype(BF16)
    n_chunks = w1_ref.shape[1] // ff_chunk
    cols = [slice(c * ff_chunk, (c + 1) * ff_chunk) for c in range(n_chunks)]

    def hidden(c):
        return jnp.maximum(_dot(x1b, w1_ref[:, cols[c]]), 0.0)

    ff = None
    hdn = hidden(0)
    for c in range(n_chunks):
        hdn_next = hidden(c + 1) if c + 1 < n_chunks else None
        part = _dot((hdn * hdn).astype(BF16), w2_ref[cols[c], :])
        ff = part if ff is None else ff + part
        hdn = hdn_next
    y_ref[...] = _layer_norm(alpha * x1 + ff, g2_ref[...], b2_ref[...])


def _ffn_call(x2d, oa, ob, oc, wo, g1, b1, w1, w2, g2, b2, *, layer, alpha):
    n, d = x2d.shape
    d_ff = w1.shape[2]
    tm = min(512, n)
    ff_chunk = min(1024, d_ff)
    assert n % tm == 0 and d_ff % ff_chunk == 0
    row_spec = lambda width: pl.BlockSpec((tm, width), lambda i: (i, 0))
    return pl.pallas_call(
        functools.partial(_ffn_kernel, alpha=alpha, ff_chunk=ff_chunk),
        grid=(n // tm,),
        in_specs=[row_spec(d), row_spec(SB_WIDTH), row_spec(MLP_WIDTH), row_spec(HG_WIDTH),
                  _layer_weight_spec(wo.shape, layer), _const_spec((1, d)), _const_spec((1, d)),
                  _layer_weight_spec(w1.shape, layer), _layer_weight_spec(w2.shape, layer),
                  _const_spec((1, d)), _const_spec((1, d))],
        out_specs=row_spec(d),
        out_shape=jax.ShapeDtypeStruct((n, d), F32),
        compiler_params=pltpu.CompilerParams(dimension_semantics=("parallel",),
                                             vmem_limit_bytes=VMEM_LIMIT_BYTES),
        name="ffn",
    )(x2d, oa, ob, oc, wo, g1, b1, w1, w2, g2, b2)


def _trunk(x, past_k, past_v, hg_state, w_in, w_out, mlp_ln_g, mlp_ln_b, mlp_ws, mlp_bs,
           hg_lb_logits, hg_norm_w, ln1_g, ln1_b, w_ff1, w_ff2, ln2_g, ln2_b):
    bsz, l, d = x.shape
    depth = w_in.shape[0]
    n = bsz * l
    alpha = (2 * depth) ** 0.25
    prompt = past_k is None
    cl = MLP_CHUNK if prompt else l
    assert MLP_CHUNK % cl == 0 and l % cl == 0
    reps = MLP_CHUNK // cl

    g = jax.nn.softmax(hg_lb_logits.astype(F32), axis=0)
    cs = jnp.cumsum(g, axis=0)
    lbs = cs - cs[0:1]

    w_in_bf, w_out_bf, w_ff1_bf, w_ff2_bf = (a.astype(BF16) for a in (w_in, w_out, w_ff1, w_ff2))
    x2d = x.reshape(n, d)
    carried_kv = ()
    new_s, new_mv = [], []
    for layer in range(depth):
        lb = lbs[layer][None, :]
        wm_tiled = jnp.tile(mlp_ws[layer][:, :cl, :cl], (1, reps, reps))
        brow = jnp.tile(jnp.repeat(jnp.transpose(mlp_bs[layer][:, :cl]), HEAD_DIM, axis=1), (reps, 1))
        outs = _proj_call(x2d, w_in_bf, mlp_ln_g[layer][None], mlp_ln_b[layer][None],
                          wm_tiled, brow, jnp.log(lb), jnp.log1p(-lb), 1.0 - lb, carried_kv,
                          layer=layer, depth=depth, seq_len=l, cl=cl, emit_vn=not prompt)
        carried_kv = tuple(outs[:2])
        qa, kab, vab, ob, qs, kk, hv, lf, gt = outs[2:11]
        shp = lambda a: a.reshape(bsz, l, a.shape[-1])
        if prompt:
            oa = _attn_call(shp(qa), shp(kab), shp(vab), q_start=0)
            s0_t = jnp.zeros((bsz, HG_HEADS, HEAD_DIM, HEAD_DIM), F32)
            t = HG_CHUNK
        else:
            past_len = past_k.shape[2]
            pad = (-(past_len + l)) % KEY_BLOCK
            cat = lambda past, new: jnp.pad(
                jnp.concatenate([past.reshape(bsz, past_len, SB_WIDTH).astype(BF16), shp(new)], axis=1),
                ((0, 0), (0, pad), (0, 0)))
            oa = _attn_call(shp(qa), cat(past_k[layer], kab), cat(past_v[layer], vab), q_start=past_len)
            s0_t = jnp.swapaxes(hg_state[layer].astype(F32), -1, -2)
            t = l
            new_mv.append(outs[11].reshape(bsz, l, MLP_WIDTH))
        oc, s_t = _hgrn_call(shp(qs), shp(kk), shp(hv), shp(lf), shp(gt), hg_norm_w[layer][None], s0_t, t=t)
        x2d = _ffn_call(x2d, oa.reshape(n, SB_WIDTH), ob, oc.reshape(n, HG_WIDTH),
                        w_out_bf, ln1_g[layer][None], ln1_b[layer][None], w_ff1_bf, w_ff2_bf,
                        ln2_g[layer][None], ln2_b[layer][None], layer=layer, alpha=alpha)
        new_s.append(jnp.swapaxes(s_t, -1, -2))
    mv = None if prompt else jnp.stack(new_mv)
    if carried_kv[0].ndim == 5:
        new_k, new_v = (jnp.transpose(a, (0, 1, 4, 2, 3)) for a in carried_kv)
    else:
        new_k, new_v = (a.reshape(depth, bsz, l, SB_HEADS, HEAD_DIM) for a in carried_kv)
    return x2d.reshape(bsz, l, d), new_k, new_v, jnp.stack(new_s), mv


def kernel(x_prompt, x_sample, cache_sb_k, cache_sb_v, state_hgrn, w_in, w_out, mlp_ln_g, mlp_ln_b, mlp_ws, mlp_bs, hg_lb_logits, hg_norm_w, ln1_g, ln1_b, w_ff1, w_ff2, ln2_g, ln2_b):
    weights = (w_in, w_out, mlp_ln_g, mlp_ln_b, mlp_ws, mlp_bs, hg_lb_logits, hg_norm_w,
               ln1_g, ln1_b, w_ff1, w_ff2, ln2_g, ln2_b)
    y_p, k_p, v_p, s_p, _ = _trunk(x_prompt, None, None, None, *weights)
    y_s, k_s, v_s, s_s, mv_s = _trunk(x_sample, cache_sb_k, cache_sb_v, state_hgrn, *weights)
    return (y_p, y_s, k_p, v_p, s_p, k_s, v_s, s_s, mv_s)
```

```python
import functools
import math

import jax
import jax.numpy as jnp
from jax import lax
from jax.experimental import pallas as pl
from jax.experimental.pallas import tpu as pltpu

F32 = jnp.float32
BF16 = jnp.bfloat16

HEAD_DIM = 64
SB_HEADS = 6
SB_WIDTH = SB_HEADS * HEAD_DIM
MLP_GROUPS = 4
MLP_WIDTH = MLP_GROUPS * HEAD_DIM
MLP_CHUNK = 128
HG_HEADS = 6
HG_WIDTH = HG_HEADS * HEAD_DIM
HG_CHUNK = 64
HG_SUB = 8
HG_GROUP = 16
MXU_WIDTH = 256
KEY_BLOCK = 128
SUPER_KEYS = 256
UNDERFLOW_LOG2 = -160.0
LOG2E = 1.4426950408889634
LN_EPS = 1e-5
RMS_EPS = 1e-6

OFF_QA = 0
OFF_KA = OFF_QA + SB_WIDTH
OFF_VA = OFF_KA + SB_WIDTH
OFF_UB = OFF_VA + SB_WIDTH
OFF_VB = OFF_UB + MLP_WIDTH
OFF_QC = OFF_VB + MLP_WIDTH
OFF_FC = OFF_QC + HG_WIDTH
OFF_IC = OFF_FC + HG_WIDTH
OFF_GC = OFF_IC + HG_WIDTH
IN_WIDTH = OFF_GC + HG_WIDTH

V7X_VMEM_BYTES = 64 * 1024 * 1024
VMEM_LIMIT_BYTES = V7X_VMEM_BYTES - 8 * 1024 * 1024


def _dot(a, b):
    return jnp.dot(a, b, preferred_element_type=F32)


def _dot_nt(a, b):
    return lax.dot_general(a, b, (((1,), (1,)), ((), ())), preferred_element_type=F32)


def _sigmoid(x):
    return 1.0 / (1.0 + jnp.exp(-x))


def _log_sigmoid(x):
    return jnp.minimum(x, 0.0) - jnp.log1p(jnp.exp(-jnp.abs(x)))


def _layer_norm(x, g, b):
    mu = jnp.mean(x, axis=-1, keepdims=True)
    xc = x - mu
    var = jnp.mean(xc * xc, axis=-1, keepdims=True)
    return xc * lax.rsqrt(var + LN_EPS) * g + b


def _split_hi_lo(x):
    hi = x.astype(BF16)
    lo = (x - hi.astype(F32)).astype(BF16)
    return hi, lo


def _head_block_ones(width):
    r = lax.broadcasted_iota(jnp.int32, (width, width), 0) // HEAD_DIM
    c = lax.broadcasted_iota(jnp.int32, (width, width), 1) // HEAD_DIM
    return jnp.where(r == c, 1.0, 0.0).astype(BF16)


def _head_sums(x, ones_wide, ones_narrow):
    split = ones_wide.shape[0]
    return jnp.concatenate([_dot(x[:, :split], ones_wide), _dot(x[:, split:], ones_narrow)], axis=1)


def _layer_weight_spec(shape, layer):
    return pl.BlockSpec((None,) + tuple(shape[1:]), lambda i: (layer,) + (0,) * (len(shape) - 1),
                        pipeline_mode=pl.Buffered(1))


def _const_spec(shape):
    return pl.BlockSpec(shape, lambda i: (0,) * len(shape), pipeline_mode=pl.Buffered(1))


def _proj_kernel(x_ref, w_ref, lng_ref, lnb_ref, wm_ref, brow_ref, loglb_ref, l1mlb_ref, omlb_ref,
                 *rest, tm, cl, n_carried, layer, depth, head_major):
    (ka_ref, va_ref, qa_ref, kab_ref, vab_ref, ob_ref, qs_ref, kk_ref, hv_ref, lf_ref,
     gt_ref, *maybe_vn_ref) = rest[n_carried:]
    xb = x_ref[...].astype(BF16)

    n_tiles = pl.cdiv(IN_WIDTH, MXU_WIDTH)
    first = OFF_QC // MXU_WIDTH
    tiles = {}
    for i in list(range(first, n_tiles)) + list(range(first)):
        tiles[i] = _dot(xb, w_ref[:, i * MXU_WIDTH:min((i + 1) * MXU_WIDTH, IN_WIDTH)])

    def proj(off, width):
        parts = []
        pos = off
        while pos < off + width:
            i, lo = divmod(pos, MXU_WIDTH)
            take = min(MXU_WIDTH - lo, off + width - pos)
            parts.append(tiles[i][:, lo:lo + take])
            pos += take
        return parts[0] if len(parts) == 1 else jnp.concatenate(parts, axis=1)

    qc = proj(OFF_QC, HG_WIDTH)
    qs_ref[...] = qc * _sigmoid(qc)
    zf = proj(OFF_FC, HG_WIDTH)
    c_term = l1mlb_ref[...] + _log_sigmoid(zf)
    a_term = loglb_ref[...]
    lf_ref[...] = jnp.maximum(a_term, c_term) + jnp.log1p(jnp.exp(-jnp.abs(a_term - c_term)))
    kk_ref[...] = omlb_ref[...] * _sigmoid(-zf)
    hv_ref[...] = proj(OFF_IC, HG_WIDTH)
    gt_ref[...] = _sigmoid(proj(OFF_GC, HG_WIDTH))

    vn = _layer_norm(proj(OFF_VB, MLP_WIDTH), lng_ref[...], lnb_ref[...])
    if maybe_vn_ref:
        maybe_vn_ref[0][...] = vn
    ub = proj(OFF_UB, MLP_WIDTH)
    row = lax.broadcasted_iota(jnp.int32, (MLP_CHUNK, MLP_CHUNK), 0)
    col = lax.broadcasted_iota(jnp.int32, (MLP_CHUNK, MLP_CHUNK), 1)
    keep = (col <= row) & ((row // cl) == (col // cl))
    wm = [jnp.where(keep, wm_ref[g], 0.0).astype(BF16) for g in range(MLP_GROUPS)]
    left_head = lax.broadcasted_iota(jnp.int32, (MLP_CHUNK, 2 * HEAD_DIM), 1) < HEAD_DIM
    brow = brow_ref[...]
    vnb = vn.astype(BF16)
    for c in range(tm // MLP_CHUNK):
        rows = slice(c * MLP_CHUNK, (c + 1) * MLP_CHUNK)
        for p in range(MLP_GROUPS // 2):
            lanes = slice(p * 2 * HEAD_DIM, (p + 1) * 2 * HEAD_DIM)
            vp = vnb[rows, lanes]
            mixed = jnp.where(left_head, _dot(wm[2 * p], vp), _dot(wm[2 * p + 1], vp)) + brow[:, lanes]
            ob_ref[rows, lanes] = (ub[rows, lanes] * mixed).astype(BF16)

    qa_ref[...] = (proj(OFF_QA, SB_WIDTH) * (LOG2E / math.sqrt(HEAD_DIM))).astype(BF16)
    ka = proj(OFF_KA, SB_WIDTH)
    kab_ref[...] = ka.astype(BF16)
    va = proj(OFF_VA, SB_WIDTH)
    vab_ref[...] = va.astype(BF16)
    for src, dst in ((ka, ka_ref), (va, va_ref)):
        if n_carried == 0:
            for other in range(depth):
                if other != layer:
                    dst[other] = jnp.zeros(dst.shape[1:], F32)
            dst = dst.at[layer]
        if head_major:
            for p in range(SB_HEADS // 2):
                pair_t = jnp.transpose(src[:, p * 2 * HEAD_DIM:(p + 1) * 2 * HEAD_DIM])
                dst[2 * p] = pair_t[:HEAD_DIM]
                dst[2 * p + 1] = pair_t[HEAD_DIM:]
        else:
            dst[...] = src


def _proj_call(x2d, w_in_bf, ln_g, ln_b, wm_tiled, brow, loglb, l1mlb, omlb, carried_kv, *,
               layer, depth, seq_len, cl, emit_vn):
    n, d = x2d.shape
    tm = min(512, n)
    assert n % tm == 0 and tm % MLP_CHUNK == 0 and MLP_CHUNK % cl == 0
    head_major = seq_len % tm == 0
    first = not carried_kv
    row_spec = lambda width: pl.BlockSpec((tm, width), lambda i: (i, 0))
    layer_dim = depth if first else None
    layer_idx = 0 if first else layer
    if head_major:
        per_seq = seq_len // tm
        kv_shape = (depth, n // seq_len, SB_HEADS, HEAD_DIM, seq_len)
        layer_spec = pl.BlockSpec((layer_dim, None, SB_HEADS, HEAD_DIM, tm),
                                  lambda i: (layer_idx, i // per_seq, 0, 0, i % per_seq))
    else:
        kv_shape = (depth, n, SB_WIDTH)
        layer_spec = pl.BlockSpec((layer_dim, tm, SB_WIDTH), lambda i: (layer_idx, i, 0))
    out_shapes = [
        jax.ShapeDtypeStruct(kv_shape, F32),
        jax.ShapeDtypeStruct(kv_shape, F32),
        jax.ShapeDtypeStruct((n, SB_WIDTH), BF16),
        jax.ShapeDtypeStruct((n, SB_WIDTH), BF16),
        jax.ShapeDtypeStruct((n, SB_WIDTH), BF16),
        jax.ShapeDtypeStruct((n, MLP_WIDTH), BF16),
        jax.ShapeDtypeStruct((n, HG_WIDTH), F32),
        jax.ShapeDtypeStruct((n, HG_WIDTH), F32),
        jax.ShapeDtypeStruct((n, HG_WIDTH), F32),
        jax.ShapeDtypeStruct((n, HG_WIDTH), F32),
        jax.ShapeDtypeStruct((n, HG_WIDTH), F32),
    ]
    if emit_vn:
        out_shapes.append(jax.ShapeDtypeStruct((n, MLP_WIDTH), F32))
    out_specs = [layer_spec, layer_spec] + [row_spec(s.shape[1]) for s in out_shapes[2:]]
    n_fixed = 9
    return pl.pallas_call(
        functools.partial(_proj_kernel, tm=tm, cl=cl, n_carried=len(carried_kv), layer=layer, depth=depth,
                          head_major=head_major),
        grid=(n // tm,),
        in_specs=[
            row_spec(d),
            _layer_weight_spec(w_in_bf.shape, layer),
            _const_spec((1, MLP_WIDTH)), _const_spec((1, MLP_WIDTH)),
            _const_spec((MLP_GROUPS, MLP_CHUNK, MLP_CHUNK)),
            _const_spec((MLP_CHUNK, MLP_WIDTH)),
            _const_spec((1, HG_WIDTH)), _const_spec((1, HG_WIDTH)), _const_spec((1, HG_WIDTH)),
        ] + [pl.BlockSpec(memory_space=pl.ANY)] * len(carried_kv),
        out_specs=out_specs,
        out_shape=out_shapes,
        input_output_aliases={n_fixed + j: j for j in range(len(carried_kv))},
        compiler_params=pltpu.CompilerParams(dimension_semantics=("parallel",),
                                             vmem_limit_bytes=VMEM_LIMIT_BYTES),
        name="proj",
    )(x2d, w_in_bf, ln_g, ln_b, wm_tiled, brow, loglb, l1mlb, omlb, *carried_kv)


def _sb_scores(z, mask, uu):
    kb = KEY_BLOCK
    log_beta = jnp.minimum(z, 0.0) - jnp.log2(1.0 + jnp.exp2(-jnp.abs(z)))
    log_keep = log_beta - z
    if mask is not None:
        log_keep = jnp.where(mask, log_keep, 0.0)
    hi, lo = _split_hi_lo(log_keep)
    sums = _dot(jnp.concatenate([hi, lo], axis=1), uu)
    return log_beta + sums[:, :kb], sums[:, kb:]


def _sb_finish(pre, total, mask, run):
    w = jnp.exp2(pre + run)
    if mask is not None:
        w = jnp.where(mask, w, 0.0)
    return w.astype(BF16), run + total


def _attn_kernel(q_ref, k_ref, v_ref, o_ref, run_ref, acc_ref, *, tq, nstrip, nsuper, q_start):
    kb = KEY_BLOCK
    r = lax.broadcasted_iota(jnp.int32, (2 * kb, 2 * kb), 0) % kb
    c = lax.broadcasted_iota(jnp.int32, (2 * kb, 2 * kb), 1)
    uu = jnp.where((c >= kb) | (r > c), 1.0, 0.0).astype(BF16)
    n_rows = nstrip * tq
    row = lax.broadcasted_iota(jnp.int32, (n_rows, kb), 0)
    col = lax.broadcasted_iota(jnp.int32, (n_rows, kb), 1)
    own_mask = (row >= tq) | (col < row)
    heads = [slice(h * HEAD_DIM, (h + 1) * HEAD_DIM) for h in range(SB_HEADS)]
    n_before = q_start // SUPER_KEYS

    nblk = SUPER_KEYS // kb

    def aligned(x, m):
        return x if isinstance(x, int) else pl.multiple_of(x, m)

    def q_super(qi, has_earlier):
        q0 = aligned(qi * n_rows, n_rows)
        n_earlier = n_before + qi
        kd0 = aligned(n_earlier * SUPER_KEYS, SUPER_KEYS)
        run_ref[...] = jnp.zeros_like(run_ref)
        acc_ref[...] = jnp.zeros_like(acc_ref)

        def own_logits(g):
            lo = g * tq
            return [_dot_nt(q_ref[0, pl.ds(q0 + lo, n_rows - lo), lanes],
                            k_ref[0, pl.ds(kd0 + g * kb, kb), lanes]) for lanes in heads]

        def own_scores(g, logits):
            return [_sb_scores(z, own_mask[:n_rows - g * tq], uu) for z in logits]

        def own_finish(g, scores):
            lo = g * tq
            for h, lanes in enumerate(heads):
                pre, total = scores[h]
                w, run = _sb_finish(pre, total, own_mask[:n_rows - lo], run_ref[h, lo:, :])
                run_ref[h, lo:, :] = run
                acc_ref[h, lo:, :] += _dot(w, v_ref[0, pl.ds(kd0 + g * kb, kb), lanes])

        def earlier_logits(k0):
            return [_dot_nt(q_ref[0, pl.ds(q0, n_rows), lanes], k_ref[0, pl.ds(k0, SUPER_KEYS), lanes])
                    for lanes in heads]

        def earlier_scores(logits):
            return [[_sb_scores(z[:, g * kb:(g + 1) * kb], None, uu) for g in range(nblk)] for z in logits]

        def earlier_finish(k0, scores):
            for h, lanes in enumerate(heads):
                run = run_ref[h]
                ws = [None] * nblk
                for g in reversed(range(nblk)):
                    ws[g], run = _sb_finish(*scores[h][g], None, run)
                run_ref[h] = run
                acc_ref[h] += _dot(jnp.concatenate(ws, axis=1), v_ref[0, pl.ds(k0, SUPER_KEYS), lanes])

        k_prev = pl.multiple_of((n_earlier - 1) * SUPER_KEYS, SUPER_KEYS) if has_earlier else None
        own = [own_logits(g) for g in range(nstrip)]
        prev = earlier_logits(k_prev) if has_earlier else None
        own = [own_scores(g, own[g]) for g in range(nstrip)]
        prev = earlier_scores(prev) if has_earlier else None
        for g in reversed(range(nstrip)):
            own_finish(g, own[g])
        if has_earlier:
            earlier_finish(k_prev, prev)

            def alive(c):
                return (c[0] < n_earlier) & (c[1] > UNDERFLOW_LOG2)

            def k_super(c):
                k0 = pl.multiple_of((n_earlier - 1 - c[0]) * SUPER_KEYS, SUPER_KEYS)
                earlier_finish(k0, earlier_scores(earlier_logits(k0)))
                return c[0] + 1, jnp.max(run_ref[...])

            lax.while_loop(alive, k_super, (jnp.int32(1), jnp.max(run_ref[...])))
        o_ref[0, pl.ds(q0, n_rows), :] = jnp.concatenate(
            [acc_ref[h] for h in range(SB_HEADS)], axis=1).astype(BF16)

    def q_super_step(qi, carry):
        q_super(qi, True)
        return carry

    if n_before == 0:
        q_super(0, False)
        lax.fori_loop(1, nsuper, q_super_step, 0)
    else:
        lax.fori_loop(0, nsuper, q_super_step, 0)


def _attn_call(q, k, v, *, q_start):
    b, lq, _ = q.shape
    lk = k.shape[1]
    if lq % SUPER_KEYS == 0:
        tq, nstrip, nsuper = KEY_BLOCK, SUPER_KEYS // KEY_BLOCK, lq // SUPER_KEYS
    else:
        tq, nstrip, nsuper = lq, 1, 1
    assert tq <= KEY_BLOCK and q_start % SUPER_KEYS == 0 and q_start + lq <= lk and lk % KEY_BLOCK == 0
    spec = lambda l: pl.BlockSpec((1, l, SB_WIDTH), lambda i: (i, 0, 0))
    return pl.pallas_call(
        functools.partial(_attn_kernel, tq=tq, nstrip=nstrip, nsuper=nsuper, q_start=q_start),
        grid=(b,),
        in_specs=[spec(lq), spec(lk), spec(lk)],
        out_specs=spec(lq),
        out_shape=jax.ShapeDtypeStruct((b, lq, SB_WIDTH), BF16),
        scratch_shapes=[pltpu.VMEM((SB_HEADS, nstrip * tq, KEY_BLOCK), F32),
                        pltpu.VMEM((SB_HEADS, nstrip * tq, HEAD_DIM), F32)],
        compiler_params=pltpu.CompilerParams(dimension_semantics=("parallel",),
                                             vmem_limit_bytes=VMEM_LIMIT_BYTES),
        name="attn",
    )(q, k, v)


def _hgrn_kernel(qs_ref, kk_ref, hv_ref, lf_ref, gt_ref, nw_ref, s0_ref, o_ref, st_ref, *, rb, t, n_group):
    nsb = t // HG_SUB
    sub = HG_SUB
    w = HG_WIDTH
    heads = [slice(h * HEAD_DIM, (h + 1) * HEAD_DIM) for h in range(HG_HEADS)]

    @pl.when(pl.program_id(1) == 0)
    def _():
        st_ref[...] = s0_ref[...]

    ones_wide = _head_block_ones(MXU_WIDTH)
    ones_narrow = _head_block_ones(w - MXU_WIDTH)
    tri_incl = jnp.where(lax.broadcasted_iota(jnp.int32, (t, t), 1) <= lax.broadcasted_iota(jnp.int32, (t, t), 0),
                         1.0, 0.0).astype(BF16)
    sub_row = lax.broadcasted_iota(jnp.int32, (nsb, sub, w), 1)
    if nsb > 1:
        n_stack = sub * (nsb * (nsb - 1) // 2)
        seg_start = [sub * (i * (i - 1) // 2) for i in range(nsb + 1)]
        tr = lax.broadcasted_iota(jnp.int32, (t, n_stack), 0) // sub
        tc = lax.broadcasted_iota(jnp.int32, (t, n_stack), 1)
        pair = jnp.zeros((t, n_stack), jnp.bool_)
        for i in range(1, nsb):
            pair = pair | ((tr == i) & (tc >= seg_start[i]) & (tc < seg_start[i + 1]))

    def group(gi, carry):
        chunks = range(n_group)
        rows = [pl.ds(pl.multiple_of((gi * n_group + j) * t, t), t) for j in chunks]
        q = [qs_ref[0, r, :] for r in rows]
        k = [kk_ref[0, r, :] for r in rows]
        v = [hv_ref[0, r, :] for r in rows]

        b = []
        for r in rows:
            lf_hi, lf_lo = _split_hi_lo(lf_ref[0, r, :])
            b.append(_dot(tri_incl, lf_hi) + _dot(tri_incl, lf_lo))

        q_in, k_st, decay = [], [], []
        for j in chunks:
            b_last = b[j][t - 1:t, :]
            q_in.append((q[j] * jnp.exp(b[j])).astype(BF16))
            k_st.append((k[j] * jnp.exp(b_last - b[j])).astype(BF16))
            decay.append(jnp.exp(b_last))

        o_off = [None] * n_group
        if nsb > 1:
            qt, kt, vst = [], [], []
            for j in chunks:
                qt_parts = [jnp.zeros((sub, w), F32)]
                kt_parts = []
                vs_parts = []
                for i in range(1, nsb):
                    ref_row = b[j][i * sub - 1:i * sub, :]
                    qt_parts.append(q[j][i * sub:(i + 1) * sub, :]
                                    * jnp.exp(b[j][i * sub:(i + 1) * sub, :] - ref_row))
                    kt_parts.append(k[j][:i * sub, :] * jnp.exp(ref_row - b[j][:i * sub, :]))
                    vs_parts.append(v[j][:i * sub, :])
                qt.append(jnp.concatenate(qt_parts, axis=0).astype(BF16))
                kt.append(jnp.concatenate(kt_parts, axis=0).astype(BF16))
                vst.append(jnp.concatenate(vs_parts, axis=0).astype(BF16))
            sc = [[_dot_nt(qt[j][:, lanes], kt[j][:, lanes]) for lanes in heads] for j in chunks]
            sc = [[jnp.where(pair, x, 0.0).astype(BF16) for x in row] for row in sc]
            o_off = [jnp.concatenate([_dot(sc[j][h], vst[j][:, lanes]) for h, lanes in enumerate(heads)], axis=1)
                     for j in chunks]

        upd = [[_dot(jnp.transpose(v[j][:, lanes]).astype(BF16), k_st[j][:, lanes]) for lanes in heads]
               for j in chunks]
        states = [[st_ref[0, h] for h in range(HG_HEADS)]]
        for j in chunks:
            states.append([states[j][h] * decay[j][:, lanes] + upd[j][h] for h, lanes in enumerate(heads)])
        for h in range(HG_HEADS):
            st_ref[0, h] = states[n_group][h]
        o_state = [jnp.concatenate([_dot_nt(q_in[j][:, lanes], states[j][h].astype(BF16))
                                    for h, lanes in enumerate(heads)], axis=1) for j in chunks]

        p_all = []
        for j in chunks:
            b3 = b[j].reshape(nsb, sub, w)
            q3 = q[j].reshape(nsb, sub, w)
            k3 = k[j].reshape(nsb, sub, w)
            terms = []
            for s in range(sub):
                bs = jnp.broadcast_to(b3[:, s:s + 1, :], (nsb, sub, w))
                ks = jnp.broadcast_to(k3[:, s:s + 1, :], (nsb, sub, w))
                e = jnp.exp(b3 - bs)
                if s > 0:
                    e = jnp.where(sub_row >= s, e, 0.0)
                terms.append((q3 * ks * e).reshape(nsb * sub, w))
            a_all = jnp.concatenate(terms, axis=0).astype(BF16)
            p_all.append(_head_sums(a_all, ones_wide, ones_narrow))

        for j in chunks:
            v3 = v[j].reshape(nsb, sub, w)
            o = jnp.zeros((nsb, sub, w), F32)
            for s in range(sub):
                vs = jnp.broadcast_to(v3[:, s:s + 1, :], (nsb, sub, w))
                o = o + p_all[j][s * t:(s + 1) * t, :].reshape(nsb, sub, w) * vs
            o = o.reshape(t, w) + o_state[j]
            if nsb > 1:
                o = o + o_off[j]
            sq_hi, sq_lo = _split_hi_lo(o * o)
            ms = (_head_sums(sq_hi, ones_wide, ones_narrow)
                  + _head_sums(sq_lo, ones_wide, ones_narrow)) * (1.0 / HEAD_DIM)
            o_ref[0, rows[j], :] = (o * lax.rsqrt(ms + RMS_EPS) * nw_ref[...]
                                    * gt_ref[0, rows[j], :]).astype(BF16)
        return carry

    lax.fori_loop(0, rb // (t * n_group), group, 0)


def _hgrn_call(qs, kk, hv, lf, gt, norm_w, s0_t, *, t):
    b, l, w = qs.shape
    rb = min(HG_GROUP * HG_CHUNK, l)
    n_group = min(HG_GROUP, rb // t)
    assert l % rb == 0 and rb % (t * n_group) == 0 and t % HG_SUB == 0
    row_spec = pl.BlockSpec((1, rb, w), lambda i, j: (i, j, 0))
    st_spec = pl.BlockSpec((1, HG_HEADS, HEAD_DIM, HEAD_DIM), lambda i, j: (i, 0, 0, 0))
    return pl.pallas_call(
        functools.partial(_hgrn_kernel, rb=rb, t=t, n_group=n_group),
        grid=(b, l // rb),
        in_specs=[row_spec] * 5 + [pl.BlockSpec((1, w), lambda i, j: (0, 0)), st_spec],
        out_specs=[row_spec, st_spec],
        out_shape=[jax.ShapeDtypeStruct((b, l, w), BF16),
                   jax.ShapeDtypeStruct((b, HG_HEADS, HEAD_DIM, HEAD_DIM), F32)],
        compiler_params=pltpu.CompilerParams(dimension_semantics=("parallel", "arbitrary"),
                                             vmem_limit_bytes=VMEM_LIMIT_BYTES),
        name="hgrn",
    )(qs, kk, hv, lf, gt, norm_w, s0_t)


def _ffn_kernel(x_ref, oa_ref, ob_ref, oc_ref, wo_ref, g1_ref, b1_ref, w1_ref, w2_ref, g2_ref, b2_ref,
                y_ref, *, alpha, ff_chunk):
    tm = x_ref.shape[0]
    halves = [slice(0, tm // 2), slice(tm // 2, tm)]
    mix_in = jnp.concatenate([oa_ref[...], ob_ref[...], oc_ref[...]], axis=1)
    mix = [_dot(mix_in[h], wo_ref[...]) for h in halves]
    x1 = [_layer_norm(alpha * x_ref[h, :] + m, g1_ref[...], b1_ref[...]) for h, m in zip(halves, mix)]
    x1b = jnp.concatenate([a.astype(BF16) for a in x1], axis=0)
    n_chunks = w1_ref.shape[1] // ff_chunk
    cols = [slice(c * ff_chunk, (c + 1) * ff_chunk) for c in range(n_chunks)]

    def hidden(c):
        return jnp.maximum(_dot(x1b, w1_ref[:, cols[c]]), 0.0)

    ff = None
    hdn = hidden(0)
    for c in range(n_chunks - 1):
        hdn_next = hidden(c + 1)
        part = _dot((hdn * hdn).astype(BF16), w2_ref[cols[c], :])
        ff = part if ff is None else ff + part
        hdn = hdn_next
    sq = (hdn * hdn).astype(BF16)
    for h, a in zip(halves, x1):
        last = _dot(sq[h], w2_ref[cols[n_chunks - 1], :])
        y_ref[h, :] = _layer_norm(alpha * a + (last if ff is None else ff[h] + last), g2_ref[...], b2_ref[...])


def _ffn_call(x2d, oa, ob, oc, wo, g1, b1, w1, w2, g2, b2, *, layer, alpha):
    n, d = x2d.shape
    d_ff = w1.shape[2]
    tm = min(512, n)
    ff_chunk = min(1024, d_ff)
    assert n % tm == 0 and d_ff % ff_chunk == 0
    row_spec = lambda width: pl.BlockSpec((tm, width), lambda i: (i, 0))
    return pl.pallas_call(
        functools.partial(_ffn_kernel, alpha=alpha, ff_chunk=ff_chunk),
        grid=(n // tm,),
        in_specs=[row_spec(d), row_spec(SB_WIDTH), row_spec(MLP_WIDTH), row_spec(HG_WIDTH),
                  _layer_weight_spec(wo.shape, layer), _const_spec((1, d)), _const_spec((1, d)),
                  _layer_weight_spec(w1.shape, layer), _layer_weight_spec(w2.shape, layer),
                  _const_spec((1, d)), _const_spec((1, d))],
        out_specs=row_spec(d),
        out_shape=jax.ShapeDtypeStruct((n, d), F32),
        compiler_params=pltpu.CompilerParams(dimension_semantics=("parallel",),
                                             vmem_limit_bytes=VMEM_LIMIT_BYTES),
        name="ffn",
    )(x2d, oa, ob, oc, wo, g1, b1, w1, w2, g2, b2)


def _trunk(x, past_k, past_v, hg_state, w_in, w_out, mlp_ln_g, mlp_ln_b, mlp_ws, mlp_bs,
           hg_lb_logits, hg_norm_w, ln1_g, ln1_b, w_ff1, w_ff2, ln2_g, ln2_b):
    bsz, l, d = x.shape
    depth = w_in.shape[0]
    n = bsz * l
    alpha = (2 * depth) ** 0.25
    prompt = past_k is None
    cl = MLP_CHUNK if prompt else l
    assert MLP_CHUNK % cl == 0 and l % cl == 0
    reps = MLP_CHUNK // cl

    g = jax.nn.softmax(hg_lb_logits.astype(F32), axis=0)
    cs = jnp.cumsum(g, axis=0)
    lbs = cs - cs[0:1]

    w_in_bf, w_out_bf, w_ff1_bf, w_ff2_bf = (a.astype(BF16) for a in (w_in, w_out, w_ff1, w_ff2))
    x2d = x.reshape(n, d)
    carried_kv = ()
    new_s, new_mv = [], []
    for layer in range(depth):
        lb = lbs[layer][None, :]
        wm_tiled = jnp.tile(mlp_ws[layer][:, :cl, :cl], (1, reps, reps))
        brow = jnp.tile(jnp.repeat(jnp.transpose(mlp_bs[layer][:, :cl]), HEAD_DIM, axis=1), (reps, 1))
        outs = _proj_call(x2d, w_in_bf, mlp_ln_g[layer][None], mlp_ln_b[layer][None],
                          wm_tiled, brow, jnp.log(lb), jnp.log1p(-lb), 1.0 - lb, carried_kv,
                          layer=layer, depth=depth, seq_len=l, cl=cl, emit_vn=not prompt)
        carried_kv = tuple(outs[:2])
        qa, kab, vab, ob, qs, kk, hv, lf, gt = outs[2:11]
        shp = lambda a: a.reshape(bsz, l, a.shape[-1])
        if prompt:
            oa = _attn_call(shp(qa), shp(kab), shp(vab), q_start=0)
            s0_t = jnp.zeros((bsz, HG_HEADS, HEAD_DIM, HEAD_DIM), F32)
            t = HG_CHUNK
        else:
            past_len = past_k.shape[2]
            pad = (-(past_len + l)) % KEY_BLOCK
            cat = lambda past, new: jnp.pad(
                jnp.concatenate([past.reshape(bsz, past_len, SB_WIDTH).astype(BF16), shp(new)], axis=1),
                ((0, 0), (0, pad), (0, 0)))
            oa = _attn_call(shp(qa), cat(past_k[layer], kab), cat(past_v[layer], vab), q_start=past_len)
            s0_t = jnp.swapaxes(hg_state[layer].astype(F32), -1, -2)
            t = l
            new_mv.append(outs[11].reshape(bsz, l, MLP_WIDTH))
        oc, s_t = _hgrn_call(shp(qs), shp(kk), shp(hv), shp(lf), shp(gt), hg_norm_w[layer][None], s0_t, t=t)
        x2d = _ffn_call(x2d, oa.reshape(n, SB_WIDTH), ob, oc.reshape(n, HG_WIDTH),
                        w_out_bf, ln1_g[layer][None], ln1_b[layer][None], w_ff1_bf, w_ff2_bf,
                        ln2_g[layer][None], ln2_b[layer][None], layer=layer, alpha=alpha)
        new_s.append(jnp.swapaxes(s_t, -1, -2))
    mv = None if prompt else jnp.stack(new_mv)
    if carried_kv[0].ndim == 5:
        new_k, new_v = (jnp.transpose(a, (0, 1, 4, 2, 3)) for a in carried_kv)
    else:
        new_k, new_v = (a.reshape(depth, bsz, l, SB_HEADS, HEAD_DIM) for a in carried_kv)
    return x2d.reshape(bsz, l, d), new_k, new_v, jnp.stack(new_s), mv


def kernel(x_prompt, x_sample, cache_sb_k, cache_sb_v, state_hgrn, w_in, w_out, mlp_ln_g, mlp_ln_b, mlp_ws, mlp_bs, hg_lb_logits, hg_norm_w, ln1_g, ln1_b, w_ff1, w_ff2, ln2_g, ln2_b):
    weights = (w_in, w_out, mlp_ln_g, mlp_ln_b, mlp_ws, mlp_bs, hg_lb_logits, hg_norm_w,
               ln1_g, ln1_b, w_ff1, w_ff2, ln2_g, ln2_b)
    y_p, k_p, v_p, s_p, _ = _trunk(x_prompt, None, None, None, *weights)
    y_s, k_s, v_s, s_s, mv_s = _trunk(x_sample, cache_sb_k, cache_sb_v, state_hgrn, *weights)
    return (y_p, y_s, k_p, v_p, s_p, k_s, v_s, s_s, mv_s)
```

```python
import functools
import math

import jax
import jax.numpy as jnp
from jax import lax
from jax.experimental import pallas as pl
from jax.experimental.pallas import tpu as pltpu

F32 = jnp.float32
BF16 = jnp.bfloat16

HEAD_DIM = 64
SB_HEADS = 6
SB_WIDTH = SB_HEADS * HEAD_DIM
MLP_GROUPS = 4
MLP_WIDTH = MLP_GROUPS * HEAD_DIM
MLP_CHUNK = 128
HG_HEADS = 6
HG_WIDTH = HG_HEADS * HEAD_DIM
HG_CHUNK = 64
HG_SUB = 8
HG_GROUP = 16
MXU_WIDTH = 256
KEY_BLOCK = 128
SUPER_KEYS = 256
UNDERFLOW_LOG2 = -160.0
LOG2E = 1.4426950408889634
LN_EPS = 1e-5
RMS_EPS = 1e-6

OFF_QA = 0
OFF_KA = OFF_QA + SB_WIDTH
OFF_VA = OFF_KA + SB_WIDTH
OFF_UB = OFF_VA + SB_WIDTH
OFF_VB = OFF_UB + MLP_WIDTH
OFF_QC = OFF_VB + MLP_WIDTH
OFF_FC = OFF_QC + HG_WIDTH
OFF_IC = OFF_FC + HG_WIDTH
OFF_GC = OFF_IC + HG_WIDTH
IN_WIDTH = OFF_GC + HG_WIDTH

V7X_VMEM_BYTES = 64 * 1024 * 1024
VMEM_LIMIT_BYTES = V7X_VMEM_BYTES - 8 * 1024 * 1024


def _dot(a, b):
    return jnp.dot(a, b, preferred_element_type=F32)


def _dot_nt(a, b):
    return lax.dot_general(a, b, (((1,), (1,)), ((), ())), preferred_element_type=F32)


def _sigmoid(x):
    return 1.0 / (1.0 + jnp.exp(-x))


def _log_sigmoid(x):
    return jnp.minimum(x, 0.0) - jnp.log1p(jnp.exp(-jnp.abs(x)))


def _layer_norm(x, g, b):
    mu = jnp.mean(x, axis=-1, keepdims=True)
    xc = x - mu
    var = jnp.mean(xc * xc, axis=-1, keepdims=True)
    return xc * lax.rsqrt(var + LN_EPS) * g + b


def _split_hi_lo(x):
    hi = x.astype(BF16)
    lo = (x - hi.astype(F32)).astype(BF16)
    return hi, lo


def _head_block_ones(width):
    r = lax.broadcasted_iota(jnp.int32, (width, width), 0) // HEAD_DIM
    c = lax.broadcasted_iota(jnp.int32, (width, width), 1) // HEAD_DIM
    return jnp.where(r == c, 1.0, 0.0).astype(BF16)


def _head_sums(x, ones_wide, ones_narrow):
    split = ones_wide.shape[0]
    return jnp.concatenate([_dot(x[:, :split], ones_wide), _dot(x[:, split:], ones_narrow)], axis=1)


def _layer_weight_spec(shape, layer):
    return pl.BlockSpec((None,) + tuple(shape[1:]), lambda i: (layer,) + (0,) * (len(shape) - 1),
                        pipeline_mode=pl.Buffered(1))


def _const_spec(shape):
    return pl.BlockSpec(shape, lambda i: (0,) * len(shape), pipeline_mode=pl.Buffered(1))


def _proj_kernel(x_ref, w_ref, lng_ref, lnb_ref, wm_ref, brow_ref, loglb_ref, l1mlb_ref, omlb_ref,
                 *rest, tm, cl, n_carried, layer, depth, head_major):
    (ka_ref, va_ref, qa_ref, kab_ref, vab_ref, ob_ref, qs_ref, kk_ref, hv_ref, lf_ref,
     gt_ref, *maybe_vn_ref) = rest[n_carried:]
    xb = x_ref[...].astype(BF16)

    n_tiles = pl.cdiv(IN_WIDTH, MXU_WIDTH)
    first = OFF_QC // MXU_WIDTH
    tiles = {}
    for i in list(range(first, n_tiles)) + list(range(first)):
        tiles[i] = _dot(xb, w_ref[:, i * MXU_WIDTH:min((i + 1) * MXU_WIDTH, IN_WIDTH)])

    def proj(off, width):
        parts = []
        pos = off
        while pos < off + width:
            i, lo = divmod(pos, MXU_WIDTH)
            take = min(MXU_WIDTH - lo, off + width - pos)
            parts.append(tiles[i][:, lo:lo + take])
            pos += take
        return parts[0] if len(parts) == 1 else jnp.concatenate(parts, axis=1)

    qc = proj(OFF_QC, HG_WIDTH)
    qs_ref[...] = qc * _sigmoid(qc)
    zf = proj(OFF_FC, HG_WIDTH)
    c_term = l1mlb_ref[...] + _log_sigmoid(zf)
    a_term = loglb_ref[...]
    lf_ref[...] = jnp.maximum(a_term, c_term) + jnp.log1p(jnp.exp(-jnp.abs(a_term - c_term)))
    kk_ref[...] = omlb_ref[...] * _sigmoid(-zf)
    hv_ref[...] = proj(OFF_IC, HG_WIDTH)
    gt_ref[...] = _sigmoid(proj(OFF_GC, HG_WIDTH))

    vn = _layer_norm(proj(OFF_VB, MLP_WIDTH), lng_ref[...], lnb_ref[...])
    if maybe_vn_ref:
        maybe_vn_ref[0][...] = vn
    ub = proj(OFF_UB, MLP_WIDTH)
    row = lax.broadcasted_iota(jnp.int32, (MLP_CHUNK, MLP_CHUNK), 0)
    col = lax.broadcasted_iota(jnp.int32, (MLP_CHUNK, MLP_CHUNK), 1)
    keep = (col <= row) & ((row // cl) == (col // cl))
    wm = [jnp.where(keep, wm_ref[g], 0.0).astype(BF16) for g in range(MLP_GROUPS)]
    left_head = lax.broadcasted_iota(jnp.int32, (MLP_CHUNK, 2 * HEAD_DIM), 1) < HEAD_DIM
    brow = brow_ref[...]
    vnb = vn.astype(BF16)
    for c in range(tm // MLP_CHUNK):
        rows = slice(c * MLP_CHUNK, (c + 1) * MLP_CHUNK)
        for p in range(MLP_GROUPS // 2):
            lanes = slice(p * 2 * HEAD_DIM, (p + 1) * 2 * HEAD_DIM)
            vp = vnb[rows, lanes]
            mixed = jnp.where(left_head, _dot(wm[2 * p], vp), _dot(wm[2 * p + 1], vp)) + brow[:, lanes]
            ob_ref[rows, lanes] = (ub[rows, lanes] * mixed).astype(BF16)

    qa_ref[...] = (proj(OFF_QA, SB_WIDTH) * (LOG2E / math.sqrt(HEAD_DIM))).astype(BF16)
    ka = proj(OFF_KA, SB_WIDTH)
    kab_ref[...] = ka.astype(BF16)
    va = proj(OFF_VA, SB_WIDTH)
    vab_ref[...] = va.astype(BF16)
    for src, dst in ((ka, ka_ref), (va, va_ref)):
        if n_carried == 0:
            for other in range(depth):
                if other != layer:
                    dst[other] = jnp.zeros(dst.shape[1:], F32)
            dst = dst.at[layer]
        if head_major:
            for p in range(SB_HEADS // 2):
                pair_t = jnp.transpose(src[:, p * 2 * HEAD_DIM:(p + 1) * 2 * HEAD_DIM])
                dst[2 * p] = pair_t[:HEAD_DIM]
                dst[2 * p + 1] = pair_t[HEAD_DIM:]
        else:
            dst[...] = src


def _proj_call(x2d, w_in_bf, ln_g, ln_b, wm_tiled, brow, loglb, l1mlb, omlb, carried_kv, *,
               layer, depth, seq_len, cl, emit_vn):
    n, d = x2d.shape
    tm = min(512, n)
    assert n % tm == 0 and tm % MLP_CHUNK == 0 and MLP_CHUNK % cl == 0
    head_major = seq_len % tm == 0
    first = not carried_kv
    row_spec = lambda width: pl.BlockSpec((tm, width), lambda i: (i, 0))
    layer_dim = depth if first else None
    layer_idx = 0 if first else layer
    if head_major:
        per_seq = seq_len // tm
        kv_shape = (depth, n // seq_len, SB_HEADS, HEAD_DIM, seq_len)
        layer_spec = pl.BlockSpec((layer_dim, None, SB_HEADS, HEAD_DIM, tm),
                                  lambda i: (layer_idx, i // per_seq, 0, 0, i % per_seq))
    else:
        kv_shape = (depth, n, SB_WIDTH)
        layer_spec = pl.BlockSpec((layer_dim, tm, SB_WIDTH), lambda i: (layer_idx, i, 0))
    out_shapes = [
        jax.ShapeDtypeStruct(kv_shape, F32),
        jax.ShapeDtypeStruct(kv_shape, F32),
        jax.ShapeDtypeStruct((n, SB_WIDTH), BF16),
        jax.ShapeDtypeStruct((n, SB_WIDTH), BF16),
        jax.ShapeDtypeStruct((n, SB_WIDTH), BF16),
        jax.ShapeDtypeStruct((n, MLP_WIDTH), BF16),
        jax.ShapeDtypeStruct((n, HG_WIDTH), F32),
        jax.ShapeDtypeStruct((n, HG_WIDTH), F32),
        jax.ShapeDtypeStruct((n, HG_WIDTH), F32),
        jax.ShapeDtypeStruct((n, HG_WIDTH), F32),
        jax.ShapeDtypeStruct((n, HG_WIDTH), F32),
    ]
    if emit_vn:
        out_shapes.append(jax.ShapeDtypeStruct((n, MLP_WIDTH), F32))
    out_specs = [layer_spec, layer_spec] + [row_spec(s.shape[1]) for s in out_shapes[2:]]
    n_fixed = 9
    return pl.pallas_call(
        functools.partial(_proj_kernel, tm=tm, cl=cl, n_carried=len(carried_kv), layer=layer, depth=depth,
                          head_major=head_major),
        grid=(n // tm,),
        in_specs=[
            row_spec(d),
            _layer_weight_spec(w_in_bf.shape, layer),
            _const_spec((1, MLP_WIDTH)), _const_spec((1, MLP_WIDTH)),
            _const_spec((MLP_GROUPS, MLP_CHUNK, MLP_CHUNK)),
            _const_spec((MLP_CHUNK, MLP_WIDTH)),
            _const_spec((1, HG_WIDTH)), _const_spec((1, HG_WIDTH)), _const_spec((1, HG_WIDTH)),
        ] + [pl.BlockSpec(memory_space=pl.ANY)] * len(carried_kv),
        out_specs=out_specs,
        out_shape=out_shapes,
        input_output_aliases={n_fixed + j: j for j in range(len(carried_kv))},
        compiler_params=pltpu.CompilerParams(dimension_semantics=("parallel",),
                                             vmem_limit_bytes=VMEM_LIMIT_BYTES),
        name="proj",
    )(x2d, w_in_bf, ln_g, ln_b, wm_tiled, brow, loglb, l1mlb, omlb, *carried_kv)


def _sb_scores(z, mask, uu):
    kb = KEY_BLOCK
    log_beta = jnp.minimum(z, 0.0) - jnp.log2(1.0 + jnp.exp2(-jnp.abs(z)))
    log_keep = log_beta - z
    if mask is not None:
        log_keep = jnp.where(mask, log_keep, 0.0)
    hi, lo = _split_hi_lo(log_keep)
    sums = _dot(jnp.concatenate([hi, lo], axis=1), uu)
    return log_beta + sums[:, :kb], sums[:, kb:]


def _sb_finish(pre, total, mask, run):
    w = jnp.exp2(pre + run)
    if mask is not None:
        w = jnp.where(mask, w, 0.0)
    return w.astype(BF16), run + total


def _attn_kernel(q_ref, k_ref, v_ref, o_ref, run_ref, acc_ref, *, tq, nstrip, nsuper, q_start):
    kb = KEY_BLOCK
    r = lax.broadcasted_iota(jnp.int32, (2 * kb, 2 * kb), 0) % kb
    c = lax.broadcasted_iota(jnp.int32, (2 * kb, 2 * kb), 1)
    uu = jnp.where((c >= kb) | (r > c), 1.0, 0.0).astype(BF16)
    n_rows = nstrip * tq
    row = lax.broadcasted_iota(jnp.int32, (n_rows, kb), 0)
    col = lax.broadcasted_iota(jnp.int32, (n_rows, kb), 1)
    own_mask = (row >= tq) | (col < row)
    heads = [slice(h * HEAD_DIM, (h + 1) * HEAD_DIM) for h in range(SB_HEADS)]
    n_before = q_start // SUPER_KEYS

    nblk = SUPER_KEYS // kb

    def aligned(x, m):
        return x if isinstance(x, int) else pl.multiple_of(x, m)

    def q_super(qi, has_earlier):
        q0 = aligned(qi * n_rows, n_rows)
        n_earlier = n_before + qi
        kd0 = aligned(n_earlier * SUPER_KEYS, SUPER_KEYS)
        run_ref[...] = jnp.zeros_like(run_ref)
        acc_ref[...] = jnp.zeros_like(acc_ref)

        def own_logits(g):
            lo = g * tq
            return [_dot_nt(q_ref[0, pl.ds(q0 + lo, n_rows - lo), lanes],
                            k_ref[0, pl.ds(kd0 + g * kb, kb), lanes]) for lanes in heads]

        def own_scores(g, logits):
            return [_sb_scores(z, own_mask[:n_rows - g * tq], uu) for z in logits]

        def own_finish(g, scores):
            lo = g * tq
            for h, lanes in enumerate(heads):
                pre, total = scores[h]
                w, run = _sb_finish(pre, total, own_mask[:n_rows - lo], run_ref[h, lo:, :])
                run_ref[h, lo:, :] = run
                acc_ref[h, lo:, :] += _dot(w, v_ref[0, pl.ds(kd0 + g * kb, kb), lanes])

        def earlier_logits(k0):
            return [_dot_nt(q_ref[0, pl.ds(q0, n_rows), lanes], k_ref[0, pl.ds(k0, SUPER_KEYS), lanes])
                    for lanes in heads]

        def earlier_scores(logits):
            return [[_sb_scores(z[:, g * kb:(g + 1) * kb], None, uu) for g in range(nblk)] for z in logits]

        def earlier_finish(k0, scores):
            for h, lanes in enumerate(heads):
                run = run_ref[h]
                ws = [None] * nblk
                for g in reversed(range(nblk)):
                    ws[g], run = _sb_finish(*scores[h][g], None, run)
                run_ref[h] = run
                acc_ref[h] += _dot(jnp.concatenate(ws, axis=1), v_ref[0, pl.ds(k0, SUPER_KEYS), lanes])

        k_prev = pl.multiple_of((n_earlier - 1) * SUPER_KEYS, SUPER_KEYS) if has_earlier else None
        own = [own_logits(g) for g in range(nstrip)]
        prev = earlier_logits(k_prev) if has_earlier else None
        own = [own_scores(g, own[g]) for g in range(nstrip)]
        prev = earlier_scores(prev) if has_earlier else None
        for g in reversed(range(nstrip)):
            own_finish(g, own[g])
        if has_earlier:
            earlier_finish(k_prev, prev)

            def alive(c):
                return (c[0] < n_earlier) & (c[1] > UNDERFLOW_LOG2)

            def k_super(c):
                k0 = pl.multiple_of((n_earlier - 1 - c[0]) * SUPER_KEYS, SUPER_KEYS)
                earlier_finish(k0, earlier_scores(earlier_logits(k0)))
                return c[0] + 1, jnp.max(run_ref[...])

            lax.while_loop(alive, k_super, (jnp.int32(1), jnp.max(run_ref[...])))
        o_ref[0, pl.ds(q0, n_rows), :] = jnp.concatenate(
            [acc_ref[h] for h in range(SB_HEADS)], axis=1).astype(BF16)

    def q_super_step(qi, carry):
        q_super(qi, True)
        return carry

    if n_before == 0:
        q_super(0, False)
        lax.fori_loop(1, nsuper, q_super_step, 0)
    else:
        lax.fori_loop(0, nsuper, q_super_step, 0)


def _attn_call(q, k, v, *, q_start):
    b, lq, _ = q.shape
    lk = k.shape[1]
    if lq % SUPER_KEYS == 0:
        tq, nstrip, nsuper = KEY_BLOCK, SUPER_KEYS // KEY_BLOCK, lq // SUPER_KEYS
    else:
        tq, nstrip, nsuper = lq, 1, 1
    assert tq <= KEY_BLOCK and q_start % SUPER_KEYS == 0 and q_start + lq <= lk and lk % KEY_BLOCK == 0
    spec = lambda l: pl.BlockSpec((1, l, SB_WIDTH), lambda i: (i, 0, 0))
    return pl.pallas_call(
        functools.partial(_attn_kernel, tq=tq, nstrip=nstrip, nsuper=nsuper, q_start=q_start),
        grid=(b,),
        in_specs=[spec(lq), spec(lk), spec(lk)],
        out_specs=spec(lq),
        out_shape=jax.ShapeDtypeStruct((b, lq, SB_WIDTH), BF16),
        scratch_shapes=[pltpu.VMEM((SB_HEADS, nstrip * tq, KEY_BLOCK), F32),
                        pltpu.VMEM((SB_HEADS, nstrip * tq, HEAD_DIM), F32)],
        compiler_params=pltpu.CompilerParams(dimension_semantics=("parallel",),
                                             vmem_limit_bytes=VMEM_LIMIT_BYTES),
        name="attn",
    )(q, k, v)


def _hgrn_kernel(qs_ref, kk_ref, hv_ref, lf_ref, gt_ref, nw_ref, s0_ref, o_ref, st_ref, *, rb, t, n_group):
    nsb = t // HG_SUB
    sub = HG_SUB
    w = HG_WIDTH
    heads = [slice(h * HEAD_DIM, (h + 1) * HEAD_DIM) for h in range(HG_HEADS)]

    @pl.when(pl.program_id(1) == 0)
    def _():
        st_ref[...] = s0_ref[...]

    ones_wide = _head_block_ones(MXU_WIDTH)
    ones_narrow = _head_block_ones(w - MXU_WIDTH)
    tri_incl = jnp.where(lax.broadcasted_iota(jnp.int32, (t, t), 1) <= lax.broadcasted_iota(jnp.int32, (t, t), 0),
                         1.0, 0.0).astype(BF16)
    sub_row = lax.broadcasted_iota(jnp.int32, (nsb, sub, w), 1)
    if nsb > 1:
        n_stack = sub * (nsb * (nsb - 1) // 2)
        seg_start = [sub * (i * (i - 1) // 2) for i in range(nsb + 1)]
        tr = lax.broadcasted_iota(jnp.int32, (t, n_stack), 0) // sub
        tc = lax.broadcasted_iota(jnp.int32, (t, n_stack), 1)
        pair = jnp.zeros((t, n_stack), jnp.bool_)
        for i in range(1, nsb):
            pair = pair | ((tr == i) & (tc >= seg_start[i]) & (tc < seg_start[i + 1]))

    def group(gi, carry):
        chunks = range(n_group)
        rows = [pl.ds(pl.multiple_of((gi * n_group + j) * t, t), t) for j in chunks]
        q = [qs_ref[0, r, :] for r in rows]
        k = [kk_ref[0, r, :] for r in rows]
        v = [hv_ref[0, r, :] for r in rows]

        b = []
        for r in rows:
            lf_hi, lf_lo = _split_hi_lo(lf_ref[0, r, :])
            b.append(_dot(tri_incl, lf_hi) + _dot(tri_incl, lf_lo))

        q_in, k_st, decay = [], [], []
        for j in chunks:
            b_last = b[j][t - 1:t, :]
            q_in.append((q[j] * jnp.exp(b[j])).astype(BF16))
            k_st.append((k[j] * jnp.exp(b_last - b[j])).astype(BF16))
            decay.append(jnp.exp(b_last))

        o_off = [None] * n_group
        if nsb > 1:
            qt, kt, vst = [], [], []
            for j in chunks:
                qt_parts = [jnp.zeros((sub, w), F32)]
                kt_parts = []
                vs_parts = []
                for i in range(1, nsb):
                    ref_row = b[j][i * sub - 1:i * sub, :]
                    qt_parts.append(q[j][i * sub:(i + 1) * sub, :]
                                    * jnp.exp(b[j][i * sub:(i + 1) * sub, :] - ref_row))
                    kt_parts.append(k[j][:i * sub, :] * jnp.exp(ref_row - b[j][:i * sub, :]))
                    vs_parts.append(v[j][:i * sub, :])
                qt.append(jnp.concatenate(qt_parts, axis=0).astype(BF16))
                kt.append(jnp.concatenate(kt_parts, axis=0).astype(BF16))
                vst.append(jnp.concatenate(vs_parts, axis=0).astype(BF16))
            sc = [[_dot_nt(qt[j][:, lanes], kt[j][:, lanes]) for lanes in heads] for j in chunks]
            sc = [[jnp.where(pair, x, 0.0).astype(BF16) for x in row] for row in sc]
            o_off = [jnp.concatenate([_dot(sc[j][h], vst[j][:, lanes]) for h, lanes in enumerate(heads)], axis=1)
                     for j in chunks]

        upd = [[_dot(jnp.transpose(v[j][:, lanes]).astype(BF16), k_st[j][:, lanes]) for lanes in heads]
               for j in chunks]
        states = [[st_ref[0, h] for h in range(HG_HEADS)]]
        for j in chunks:
            states.append([states[j][h] * decay[j][:, lanes] + upd[j][h] for h, lanes in enumerate(heads)])
        for h in range(HG_HEADS):
            st_ref[0, h] = states[n_group][h]
        o_state = [jnp.concatenate([_dot_nt(q_in[j][:, lanes], states[j][h].astype(BF16))
                                    for h, lanes in enumerate(heads)], axis=1) for j in chunks]

        p_all = []
        for j in chunks:
            b3 = b[j].reshape(nsb, sub, w)
            q3 = q[j].reshape(nsb, sub, w)
            k3 = k[j].reshape(nsb, sub, w)
            terms = []
            for s in range(sub):
                bs = jnp.broadcast_to(b3[:, s:s + 1, :], (nsb, sub, w))
                ks = jnp.broadcast_to(k3[:, s:s + 1, :], (nsb, sub, w))
                e = jnp.exp(b3 - bs)
                if s > 0:
                    e = jnp.where(sub_row >= s, e, 0.0)
                terms.append((q3 * ks * e).reshape(nsb * sub, w))
            a_all = jnp.concatenate(terms, axis=0).astype(BF16)
            p_all.append(_head_sums(a_all, ones_wide, ones_narrow))

        for j in chunks:
            v3 = v[j].reshape(nsb, sub, w)
            o = jnp.zeros((nsb, sub, w), F32)
            for s in range(sub):
                vs = jnp.broadcast_to(v3[:, s:s + 1, :], (nsb, sub, w))
                o = o + p_all[j][s * t:(s + 1) * t, :].reshape(nsb, sub, w) * vs
            o = o.reshape(t, w) + o_state[j]
            if nsb > 1:
                o = o + o_off[j]
            sq_hi, sq_lo = _split_hi_lo(o * o)
            ms = (_head_sums(sq_hi, ones_wide, ones_narrow)
                  + _head_sums(sq_lo, ones_wide, ones_narrow)) * (1.0 / HEAD_DIM)
            o_ref[0, rows[j], :] = (o * lax.rsqrt(ms + RMS_EPS) * nw_ref[...]
                                    * gt_ref[0, rows[j], :]).astype(BF16)
        return carry

    lax.fori_loop(0, rb // (t * n_group), group, 0)


def _hgrn_call(qs, kk, hv, lf, gt, norm_w, s0_t, *, t):
    b, l, w = qs.shape
    rb = min(HG_GROUP * HG_CHUNK, l)
    n_group = min(HG_GROUP, rb // t)
    assert l % rb == 0 and rb % (t * n_group) == 0 and t % HG_SUB == 0
    row_spec = pl.BlockSpec((1, rb, w), lambda i, j: (i, j, 0))
    st_spec = pl.BlockSpec((1, HG_HEADS, HEAD_DIM, HEAD_DIM), lambda i, j: (i, 0, 0, 0))
    return pl.pallas_call(
        functools.partial(_hgrn_kernel, rb=rb, t=t, n_group=n_group),
        grid=(b, l // rb),
        in_specs=[row_spec] * 5 + [pl.BlockSpec((1, w), lambda i, j: (0, 0)), st_spec],
        out_specs=[row_spec, st_spec],
        out_shape=[jax.ShapeDtypeStruct((b, l, w), BF16),
                   jax.ShapeDtypeStruct((b, HG_HEADS, HEAD_DIM, HEAD_DIM), F32)],
        compiler_params=pltpu.CompilerParams(dimension_semantics=("parallel", "arbitrary"),
                                             vmem_limit_bytes=VMEM_LIMIT_BYTES),
        name="hgrn",
    )(qs, kk, hv, lf, gt, norm_w, s0_t)


def _ffn_kernel(x_ref, oa_ref, ob_ref, oc_ref, wo_ref, g1_ref, b1_ref, w1_ref, w2_ref, g2_ref, b2_ref,
                y_ref, *, alpha, ff_chunk):
    tm = x_ref.shape[0]
    n_parts = 2 if tm >= 2 * MXU_WIDTH else 1
    halves = [slice(i * tm // n_parts, (i + 1) * tm // n_parts) for i in range(n_parts)]
    mix_in = jnp.concatenate([oa_ref[...], ob_ref[...], oc_ref[...]], axis=1)
    mix = [_dot(mix_in[h], wo_ref[...]) for h in halves]
    x1 = [_layer_norm(alpha * x_ref[h, :] + m, g1_ref[...], b1_ref[...]) for h, m in zip(halves, mix)]
    x1b = [a.astype(BF16) for a in x1]
    n_chunks = w1_ref.shape[1] // ff_chunk
    cols = [slice(c * ff_chunk, (c + 1) * ff_chunk) for c in range(n_chunks)]

    def hidden(c):
        return [jnp.maximum(_dot(a, w1_ref[:, cols[c]]), 0.0) for a in x1b]

    ff = [None] * len(halves)
    hdn = hidden(0)
    for c in range(n_chunks):
        hdn_next = hidden(c + 1) if c + 1 < n_chunks else None
        for i in range(len(halves)):
            part = _dot((hdn[i] * hdn[i]).astype(BF16), w2_ref[cols[c], :])
            ff[i] = part if ff[i] is None else ff[i] + part
        hdn = hdn_next
    for h, a, f in zip(halves, x1, ff):
        y_ref[h, :] = _layer_norm(alpha * a + f, g2_ref[...], b2_ref[...])


def _ffn_call(x2d, oa, ob, oc, wo, g1, b1, w1, w2, g2, b2, *, layer, alpha):
    n, d = x2d.shape
    d_ff = w1.shape[2]
    tm = min(512, n)
    ff_chunk = min(1024, d_ff)
    assert n % tm == 0 and d_ff % ff_chunk == 0
    row_spec = lambda width: pl.BlockSpec((tm, width), lambda i: (i, 0))
    return pl.pallas_call(
        functools.partial(_ffn_kernel, alpha=alpha, ff_chunk=ff_chunk),
        grid=(n // tm,),
        in_specs=[row_spec(d), row_spec(SB_WIDTH), row_spec(MLP_WIDTH), row_spec(HG_WIDTH),
                  _layer_weight_spec(wo.shape, layer), _const_spec((1, d)), _const_spec((1, d)),
                  _layer_weight_spec(w1.shape, layer), _layer_weight_spec(w2.shape, layer),
                  _const_spec((1, d)), _const_spec((1, d))],
        out_specs=row_spec(d),
        out_shape=jax.ShapeDtypeStruct((n, d), F32),
        compiler_params=pltpu.CompilerParams(dimension_semantics=("parallel",),
                                             vmem_limit_bytes=VMEM_LIMIT_BYTES),
        name="ffn",
    )(x2d, oa, ob, oc, wo, g1, b1, w1, w2, g2, b2)


def _trunk(x, past_k, past_v, hg_state, w_in, w_out, mlp_ln_g, mlp_ln_b, mlp_ws, mlp_bs,
           hg_lb_logits, hg_norm_w, ln1_g, ln1_b, w_ff1, w_ff2, ln2_g, ln2_b):
    bsz, l, d = x.shape
    depth = w_in.shape[0]
    n = bsz * l
    alpha = (2 * depth) ** 0.25
    prompt = past_k is None
    cl = MLP_CHUNK if prompt else l
    assert MLP_CHUNK % cl == 0 and l % cl == 0
    reps = MLP_CHUNK // cl

    g = jax.nn.softmax(hg_lb_logits.astype(F32), axis=0)
    cs = jnp.cumsum(g, axis=0)
    lbs = cs - cs[0:1]

    w_in_bf, w_out_bf, w_ff1_bf, w_ff2_bf = (a.astype(BF16) for a in (w_in, w_out, w_ff1, w_ff2))
    x2d = x.reshape(n, d)
    carried_kv = ()
    new_s, new_mv = [], []
    for layer in range(depth):
        lb = lbs[layer][None, :]
        wm_tiled = jnp.tile(mlp_ws[layer][:, :cl, :cl], (1, reps, reps))
        brow = jnp.tile(jnp.repeat(jnp.transpose(mlp_bs[layer][:, :cl]), HEAD_DIM, axis=1), (reps, 1))
        outs = _proj_call(x2d, w_in_bf, mlp_ln_g[layer][None], mlp_ln_b[layer][None],
                          wm_tiled, brow, jnp.log(lb), jnp.log1p(-lb), 1.0 - lb, carried_kv,
                          layer=layer, depth=depth, seq_len=l, cl=cl, emit_vn=not prompt)
        carried_kv = tuple(outs[:2])
        qa, kab, vab, ob, qs, kk, hv, lf, gt = outs[2:11]
        shp = lambda a: a.reshape(bsz, l, a.shape[-1])
        if prompt:
            oa = _attn_call(shp(qa), shp(kab), shp(vab), q_start=0)
            s0_t = jnp.zeros((bsz, HG_HEADS, HEAD_DIM, HEAD_DIM), F32)
            t = HG_CHUNK
        else:
            past_len = past_k.shape[2]
            pad = (-(past_len + l)) % KEY_BLOCK
            cat = lambda past, new: jnp.pad(
                jnp.concatenate([past.reshape(bsz, past_len, SB_WIDTH).astype(BF16), shp(new)], axis=1),
                ((0, 0), (0, pad), (0, 0)))
            oa = _attn_call(shp(qa), cat(past_k[layer], kab), cat(past_v[layer], vab), q_start=past_len)
            s0_t = jnp.swapaxes(hg_state[layer].astype(F32), -1, -2)
            t = l
            new_mv.append(outs[11].reshape(bsz, l, MLP_WIDTH))
        oc, s_t = _hgrn_call(shp(qs), shp(kk), shp(hv), shp(lf), shp(gt), hg_norm_w[layer][None], s0_t, t=t)
        x2d = _ffn_call(x2d, oa.reshape(n, SB_WIDTH), ob, oc.reshape(n, HG_WIDTH),
                        w_out_bf, ln1_g[layer][None], ln1_b[layer][None], w_ff1_bf, w_ff2_bf,
                        ln2_g[layer][None], ln2_b[layer][None], layer=layer, alpha=alpha)
        new_s.append(jnp.swapaxes(s_t, -1, -2))
    mv = None if prompt else jnp.stack(new_mv)
    if carried_kv[0].ndim == 5:
        new_k, new_v = (jnp.transpose(a, (0, 1, 4, 2, 3)) for a in carried_kv)
    else:
        new_k, new_v = (a.reshape(depth, bsz, l, SB_HEADS, HEAD_DIM) for a in carried_kv)
    return x2d.reshape(bsz, l, d), new_k, new_v, jnp.stack(new_s), mv


def kernel(x_prompt, x_sample, cache_sb_k, cache_sb_v, state_hgrn, w_in, w_out, mlp_ln_g, mlp_ln_b, mlp_ws, mlp_bs, hg_lb_logits, hg_norm_w, ln1_g, ln1_b, w_ff1, w_ff2, ln2_g, ln2_b):
    weights = (w_in, w_out, mlp_ln_g, mlp_ln_b, mlp_ws, mlp_bs, hg_lb_logits, hg_norm_w,
               ln1_g, ln1_b, w_ff1, w_ff2, ln2_g, ln2_b)
    y_p, k_p, v_p, s_p, _ = _trunk(x_prompt, None, None, None, *weights)
    y_s, k_s, v_s, s_s, mv_s = _trunk(x_sample, cache_sb_k, cache_sb_v, state_hgrn, *weights)
    return (y_p, y_s, k_p, v_p, s_p, k_s, v_s, s_s, mv_s)
```
